```python
import jax, jax.numpy as jnp
from jax import lax
import numpy as np

D_MODEL = 1024
BATCH = 2
SEQ = 16384
DEPTH = 2
DEC_BATCH = 16
DEC_SEQ = 64
PAST_LEN = 2048

CHUNK = 64
EPS = 1e-6
D_MIX = D_MODEL
D_CONV = D_MIX // 2
CONV_W = 3
D_MLSTM = D_MIX - D_CONV
MLSTM_HEADS = 4
MLSTM_HD = D_MLSTM // MLSTM_HEADS
N_MEM = 256
MEM_HEADS = 4
MEM_HD = D_MODEL // MEM_HEADS
D_FF = 2816
FF_CONV_W = 3
D_IN = 3 * D_CONV + 4 * D_MLSTM + 2 * MLSTM_HEADS
SPLITS = (D_CONV, 2 * D_CONV, 3 * D_CONV, 3 * D_CONV + D_MLSTM, 3 * D_CONV + 2 * D_MLSTM,
          3 * D_CONV + 3 * D_MLSTM, 3 * D_CONV + 4 * D_MLSTM)

kernel_name = 'hymba_conv_mlstm_streaming_step'


def rms_norm(x, g):
    xf = x.astype(jnp.float32)
    y = xf * lax.rsqrt(jnp.mean(xf * xf, axis=-1, keepdims=True) + EPS)
    return (y * g.astype(jnp.float32)).astype(x.dtype)


def causal_dwconv(u, buf, w):
    T = u.shape[1]
    ext = jnp.concatenate([buf.astype(u.dtype), u], axis=1)
    y = ext[:, 0:T] * w[0]
    for j in range(1, w.shape[0]):
        y = y + ext[:, j:j + T] * w[j]
    return y, ext[:, T:]


def mlstm_chunkwise(q, k, v, i_pre, logf, C0, n0, m0):
    B, H, T, DH = q.shape
    L = min(CHUNK, T)
    NC = T // L

    def chunks(a):
        return jnp.moveaxis(a.reshape(a.shape[:2] + (NC, L) + a.shape[3:]), 2, 0)

    causal = jnp.tril(jnp.ones((L, L), dtype=bool))

    def step(carry, blk):
        C, n, m = carry
        qb, kb, vb, ib, fb = blk
        bcum = jnp.cumsum(fb, axis=-1)
        d_log = bcum[..., :, None] - bcum[..., None, :] + ib[..., None, :]
        d_log = jnp.where(causal, d_log, -jnp.inf)
        inter_log = bcum + m[..., None]
        m_t = jnp.maximum(jnp.max(d_log, axis=-1), inter_log)
        s = jnp.einsum('bhtd,bhsd->bhts', qb, kb) * jnp.exp(d_log - m_t[..., None])
        w_inter = jnp.exp(inter_log - m_t)
        num = jnp.einsum('bhts,bhse->bhte', s, vb) + w_inter[..., None] * jnp.einsum('bhtd,bhde->bhte', qb, C)
        den = jnp.sum(s, axis=-1) + w_inter * jnp.einsum('bhtd,bhd->bht', qb, n)
        h = num / jnp.maximum(jnp.abs(den), jnp.exp(-m_t))[..., None]
        m_new = m_t[..., -1]
        w_s = jnp.exp(bcum[..., -1:] - bcum + ib - m_new[..., None])
        w_c = jnp.exp(bcum[..., -1] + m - m_new)
        C_new = w_c[..., None, None] * C + jnp.einsum('bhs,bhsd,bhse->bhde', w_s, kb, vb)
        n_new = w_c[..., None] * n + jnp.einsum('bhs,bhsd->bhd', w_s, kb)
        return (C_new, n_new, m_new), h

    (C, n, m), hs = lax.scan(step, (C0, n0, m0),
                             (chunks(q), chunks(k), chunks(v), chunks(i_pre), chunks(logf)))
    h = jnp.moveaxis(hs, 0, 2).reshape(B, H, T, DH)
    return h, C, n, m


def token_mixer(xn, conv_buf, C0, n0, m0, w_in, b_gates, conv_w, mlstm_norm_w, w_out):
    B, T, _ = xn.shape
    f32 = jnp.float32
    xc, g_b, g_c, q, k, v, o, gates = jnp.split(xn @ w_in, SPLITS, axis=-1)
    y_a, conv_new = causal_dwconv(g_c * xc, conv_buf, conv_w)
    y_a = g_b * y_a
    def heads(a):
        return a.astype(f32).reshape(B, T, MLSTM_HEADS, MLSTM_HD).transpose(0, 2, 1, 3)
    gates = (gates + b_gates).astype(f32)
    i_pre = gates[..., :MLSTM_HEADS].transpose(0, 2, 1)
    logf = jax.nn.log_sigmoid(gates[..., MLSTM_HEADS:]).transpose(0, 2, 1)
    h, C, n, m = mlstm_chunkwise(heads(q), heads(k) * (MLSTM_HD ** -0.5), heads(v), i_pre, logf, C0, n0, m0)
    h = h * lax.rsqrt(jnp.mean(h * h, axis=-1, keepdims=True) + EPS)
    h = h.transpose(0, 2, 1, 3).reshape(B, T, D_MLSTM) * mlstm_norm_w.astype(f32)
    y_b = (jax.nn.sigmoid(o.astype(f32)) * h).astype(xn.dtype)
    y = jnp.concatenate([y_a, y_b], axis=-1) @ w_out
    return y, conv_new, C, n, m


def memory_kv(mem, g, w_k, w_v):
    B = mem.shape[0]
    mn = rms_norm(mem, g)
    mk = (mn @ w_k).reshape(B, N_MEM, MEM_HEADS, MEM_HD)
    mv = (mn @ w_v).reshape(B, N_MEM, MEM_HEADS, MEM_HD)
    return mk, mv


def memory_attend(xn, mk, mv, w_q, w_o):
    B, T, _ = xn.shape
    q = (xn @ w_q).reshape(B, T, MEM_HEADS, MEM_HD)
    s = jnp.einsum('bthd,bmhd->bhtm', q, mk).astype(jnp.float32) * (MEM_HD ** -0.5)
    p = jax.nn.softmax(s, axis=-1).astype(xn.dtype)
    o = jnp.einsum('bhtm,bmhd->bthd', p, mv).reshape(B, T, D_MODEL)
    return o @ w_o


def conv_ffn(xn, ff_buf, w_up, ffn_conv_w, w_down):
    up, ff_new = causal_dwconv(xn @ w_up, ff_buf, ffn_conv_w)
    a, g = jnp.split(up, 2, axis=-1)
    return (jax.nn.gelu(g) * a) @ w_down, ff_new


def trunk_layer(x, mk, mv, conv_buf, C0, n0, m0, ff_buf, p):
    y, conv_new, C, n, m = token_mixer(rms_norm(x, p['norm_mix_pre']), conv_buf, C0, n0, m0,
                                       p['w_in'], p['b_gates'], p['conv_w'], p['mlstm_norm_w'], p['w_out'])
    x = x + rms_norm(y, p['norm_mix_post'])
    y = memory_attend(rms_norm(x, p['norm_mem_pre']), mk, mv, p['w_mq'], p['w_mo'])
    x = x + rms_norm(y, p['norm_mem_post'])
    y, ff_new = conv_ffn(rms_norm(x, p['norm_ffn_pre']), ff_buf, p['w_up'], p['ffn_conv_w'], p['w_down'])
    x = x + rms_norm(y, p['norm_ffn_post'])
    return x, conv_new, C, n, m, ff_new


def setup_inputs(seed: int = 0) -> dict:
    key = jax.random.key(seed)
    ks = jax.random.split(key, 40)
    f32 = jnp.float32

    def nrm(k, shape, s):
        return jax.random.normal(k, shape, f32) * s

    def gain(k, shape):
        return 1.0 + 0.05 * jax.random.normal(k, shape, f32)

    b_i = 0.1 * jax.random.normal(ks[0], (DEPTH, MLSTM_HEADS), f32)
    b_f = jnp.linspace(3.0, 6.0, MLSTM_HEADS, dtype=f32)[None, :] + 0.1 * jax.random.normal(ks[1], (DEPTH, MLSTM_HEADS), f32)
    return {
        'x_prompt': nrm(ks[2], (BATCH, SEQ, D_MODEL), 1.0),
        'x_sample': nrm(ks[3], (DEC_BATCH, DEC_SEQ, D_MODEL), 1.0),
        'mem_prompt': nrm(ks[4], (BATCH, N_MEM, D_MODEL), 1.0),
        'cache_mem_k': nrm(ks[5], (DEPTH, DEC_BATCH, N_MEM, MEM_HEADS, MEM_HD), 1.0),
        'cache_mem_v': nrm(ks[6], (DEPTH, DEC_BATCH, N_MEM, MEM_HEADS, MEM_HD), 1.0),
        'state_conv': nrm(ks[7], (DEPTH, DEC_BATCH, CONV_W - 1, D_CONV), 1.0),
        'state_mlstm_C': nrm(ks[8], (DEPTH, DEC_BATCH, MLSTM_HEADS, MLSTM_HD, MLSTM_HD), 0.1),
        'state_mlstm_n': nrm(ks[9], (DEPTH, DEC_BATCH, MLSTM_HEADS, MLSTM_HD), 0.5),
        'state_mlstm_m': nrm(ks[10], (DEPTH, DEC_BATCH, MLSTM_HEADS), 0.5),
        'state_ffn_conv': nrm(ks[11], (DEPTH, DEC_BATCH, FF_CONV_W - 1, 2 * D_FF), 1.0),
        'norm_mix_pre': gain(ks[12], (DEPTH, D_MODEL)),
        'w_in': nrm(ks[13], (DEPTH, D_MODEL, D_IN), D_MODEL ** -0.5),
        'b_gates': jnp.concatenate([b_i, b_f], axis=-1),
        'conv_w': nrm(ks[14], (DEPTH, CONV_W, D_CONV), CONV_W ** -0.5),
        'mlstm_norm_w': gain(ks[15], (DEPTH, D_MLSTM)),
        'w_out': nrm(ks[16], (DEPTH, D_MIX, D_MODEL), D_MIX ** -0.5),
        'norm_mix_post': gain(ks[17], (DEPTH, D_MODEL)),
        'norm_mem_pre': gain(ks[18], (DEPTH, D_MODEL)),
        'norm_mem_kv': gain(ks[19], (DEPTH, D_MODEL)),
        'w_mq': nrm(ks[20], (DEPTH, D_MODEL, D_MODEL), D_MODEL ** -0.5),
        'w_mk': nrm(ks[21], (DEPTH, D_MODEL, D_MODEL), D_MODEL ** -0.5),
        'w_mv': nrm(ks[22], (DEPTH, D_MODEL, D_MODEL), D_MODEL ** -0.5),
        'w_mo': nrm(ks[23], (DEPTH, D_MODEL, D_MODEL), D_MODEL ** -0.5),
        'norm_mem_post': gain(ks[24], (DEPTH, D_MODEL)),
        'norm_ffn_pre': gain(ks[25], (DEPTH, D_MODEL)),
        'w_up': nrm(ks[26], (DEPTH, D_MODEL, 2 * D_FF), D_MODEL ** -0.5),
        'ffn_conv_w': nrm(ks[27], (DEPTH, FF_CONV_W, 2 * D_FF), FF_CONV_W ** -0.5),
        'w_down': nrm(ks[28], (DEPTH, D_FF, D_MODEL), D_FF ** -0.5),
        'norm_ffn_post': gain(ks[29], (DEPTH, D_MODEL)),
    }


def reference(x_prompt, x_sample, mem_prompt, cache_mem_k, cache_mem_v, state_conv, state_mlstm_C,
              state_mlstm_n, state_mlstm_m, state_ffn_conv,
              norm_mix_pre, w_in, b_gates, conv_w, mlstm_norm_w, w_out, norm_mix_post,
              norm_mem_pre, norm_mem_kv, w_mq, w_mk, w_mv, w_mo, norm_mem_post,
              norm_ffn_pre, w_up, ffn_conv_w, w_down, norm_ffn_post):
    f32 = jnp.float32
    xp, xs = x_prompt, x_sample
    B = xp.shape[0]
    mk_p_all, mv_p_all = [], []
    conv_p_all, conv_s_all = [], []
    C_p_all, C_s_all, n_p_all, n_s_all, m_p_all, m_s_all = [], [], [], [], [], []
    ff_p_all, ff_s_all = [], []
    for l in range(DEPTH):
        p = {'norm_mix_pre': norm_mix_pre[l], 'w_in': w_in[l], 'b_gates': b_gates[l], 'conv_w': conv_w[l],
             'mlstm_norm_w': mlstm_norm_w[l], 'w_out': w_out[l], 'norm_mix_post': norm_mix_post[l],
             'norm_mem_pre': norm_mem_pre[l], 'w_mq': w_mq[l], 'w_mo': w_mo[l], 'norm_mem_post': norm_mem_post[l],
             'norm_ffn_pre': norm_ffn_pre[l], 'w_up': w_up[l], 'ffn_conv_w': ffn_conv_w[l], 'w_down': w_down[l],
             'norm_ffn_post': norm_ffn_post[l]}
        mk_p, mv_p = memory_kv(mem_prompt, norm_mem_kv[l], w_mk[l], w_mv[l])
        xp, conv_p, C_p, n_p, m_p, ff_p = trunk_layer(
            xp, mk_p, mv_p,
            jnp.zeros((B, CONV_W - 1, D_CONV), xp.dtype),
            jnp.zeros((B, MLSTM_HEADS, MLSTM_HD, MLSTM_HD), f32),
            jnp.zeros((B, MLSTM_HEADS, MLSTM_HD), f32),
            jnp.zeros((B, MLSTM_HEADS), f32),
            jnp.zeros((B, FF_CONV_W - 1, 2 * D_FF), xp.dtype), p)
        xs, conv_s, C_s, n_s, m_s, ff_s = trunk_layer(
            xs, cache_mem_k[l], cache_mem_v[l], state_conv[l],
            state_mlstm_C[l].astype(f32), state_mlstm_n[l].astype(f32), state_mlstm_m[l].astype(f32),
            state_ffn_conv[l], p)
        mk_p_all.append(mk_p); mv_p_all.append(mv_p)
        conv_p_all.append(conv_p); conv_s_all.append(conv_s)
        C_p_all.append(C_p); C_s_all.append(C_s)
        n_p_all.append(n_p); n_s_all.append(n_s)
        m_p_all.append(m_p); m_s_all.append(m_s)
        ff_p_all.append(ff_p); ff_s_all.append(ff_s)
    st = jnp.stack
    return (xp, xs, st(mk_p_all), st(mv_p_all), st(conv_p_all), st(conv_s_all),
            st(C_p_all), st(C_s_all), st(n_p_all), st(n_s_all), st(m_p_all), st(m_s_all),
            st(ff_p_all), st(ff_s_all))
```

```python
import functools

import jax
import jax.numpy as jnp
from jax import lax
from jax.experimental import pallas as pl
from jax.experimental.pallas import tpu as pltpu

F32 = jnp.float32
BF16 = jnp.bfloat16

EPS = 1e-6
CONV_W = 3
MLSTM_HEADS = 4
MEM_HEADS = 4
LANES = 128
SUBLANES = 8
CONV_PAD = SUBLANES
VMEM_LIMIT_BYTES = 56 * 1024 * 1024


def _rms(x, g):
    return x * lax.rsqrt(jnp.mean(x * x, axis=-1, keepdims=True) + EPS) * g


def _dot(a, b):
    return jnp.dot(a, b, preferred_element_type=F32)


def _dot_nt(a, b):
    return lax.dot_general(a, b, (((1,), (1,)), ((), ())), preferred_element_type=F32)


def _dot_tn(a, b):
    return lax.dot_general(a, b, (((0,), (0,)), ((), ())), preferred_element_type=F32)


def _const_spec(shape):
    zeros = (0,) * len(shape)
    return pl.BlockSpec(shape, lambda b, t: zeros, pipeline_mode=pl.Buffered(1))


def _causal_conv(ext_ref, tail, u, w_ref, cols, TT):
    lo = CONV_PAD - (CONV_W - 1)
    ext_ref[:, lo:CONV_PAD, :] = tail
    ext_ref[:, CONV_PAD:CONV_PAD + TT, :] = u
    y = ext_ref[:, lo:lo + TT, :] * w_ref[0:1, cols]
    for j in range(1, CONV_W):
        y = y + ext_ref[:, lo + j:lo + j + TT, :] * w_ref[j:j + 1, cols]
    return y, ext_ref[:, TT + lo:TT + CONV_PAD, :]


def _memkv_kernel(mem_ref, g_ref, wk_ref, wv_ref, k_ref, v_ref):
    mn = _rms(mem_ref[0], g_ref[...]).astype(BF16)
    k_ref[0] = _dot(mn, wk_ref[...])
    v_ref[0] = _dot(mn, wv_ref[...])


def _memkv(mem, g, wk, wv):
    B, M, D = mem.shape
    spec = pl.BlockSpec((1, M, D), lambda b: (b, 0, 0))
    full = lambda s: pl.BlockSpec(s, lambda b: (0,) * len(s))
    return pl.pallas_call(
        _memkv_kernel,
        grid=(B,),
        in_specs=[spec, full((1, D)), full((D, D)), full((D, D))],
        out_specs=[spec, spec],
        out_shape=[jax.ShapeDtypeStruct((B, M, D), F32)] * 2,
        compiler_params=pltpu.CompilerParams(
            dimension_semantics=("arbitrary",), vmem_limit_bytes=VMEM_LIMIT_BYTES),
        name="mem_kv",
    )(mem, g, wk, wv)


def _mixer_kernel(x_ref, conv0_ref, C0_ref, n0_ref, m0_ref, gpre_ref, win_ref, wg_ref, bg_ref,
                  convw_ref, mnw_ref, wout_ref, gpost_ref,
                  xo_ref, convo_ref, Co_ref, no_ref, mo_ref,
                  ext_ref, qkv_ref, sig_ref, bc_ref, a_ref, at_ref, y_ref,
                  *, BB, TT, L, DC, HD):
    rows = BB * TT
    H = MLSTM_HEADS
    DM = H * HD
    D = x_ref.shape[-1]
    n_chunks = rows // L
    chunks_per_seq = TT // L

    @pl.when(pl.program_id(1) == 0)
    def _():
        convo_ref[...] = conv0_ref[...]
        Co_ref[...] = C0_ref[...]
        no_ref[...] = n0_ref[...]
        mo_ref[...] = m0_ref[...]

    x = x_ref[...].reshape(rows, D)
    xn = _rms(x, gpre_ref[...]).astype(BF16)

    pa = _dot(xn, win_ref[:, 0:3 * DC])
    u = (pa[:, 2 * DC:3 * DC] * pa[:, 0:DC]).reshape(BB, TT, DC)
    ya, tail = _causal_conv(ext_ref, convo_ref[...], u, convw_ref, slice(0, DC), TT)
    convo_ref[...] = tail
    y_ref[:, 0:DC] = (pa[:, DC:2 * DC] * ya.reshape(rows, DC)).astype(BF16)

    pm = _dot(xn, win_ref[:, 3 * DC:3 * DC + 4 * DM])
    qkv_ref[:, 0:DM] = pm[:, 0:DM].astype(BF16)
    qkv_ref[:, DM:2 * DM] = (pm[:, DM:2 * DM] * (HD ** -0.5)).astype(BF16)
    qkv_ref[:, 2 * DM:3 * DM] = pm[:, 2 * DM:3 * DM].astype(BF16)
    sig_ref[...] = 1.0 / (1.0 + jnp.exp(-pm[:, 3 * DM:4 * DM]))

    g = _dot(xn, wg_ref[...]) + bg_ref[...]
    ipre = g[:, 0:LANES]
    fpre = g[:, LANES:2 * LANES]
    logf = jnp.minimum(fpre, 0.0) - jnp.log1p(jnp.exp(-jnp.abs(fpre)))
    pos = lax.broadcasted_iota(jnp.int32, (rows, LANES), 0) & (L - 1)
    bc = logf
    sh = 1
    while sh < L:
        bc = bc + jnp.where(pos >= sh, pltpu.roll(bc, sh, axis=0), 0.0)
        sh *= 2
    a = ipre - bc
    a_t = a.T
    for c in range(n_chunks):
        at_ref[c] = a_t[0:SUBLANES, c * L:(c + 1) * L]
    for h in range(H):
        bc_ref[h] = jnp.broadcast_to(bc[:, h:h + 1], (rows, LANES))
        a_ref[h] = jnp.broadcast_to(a[:, h:h + 1], (rows, LANES))

    causal = (lax.broadcasted_iota(jnp.int32, (L, L), 0)
              >= lax.broadcasted_iota(jnp.int32, (L, L), 1))

    def chunk(ci, carry):
        r0 = pl.multiple_of(ci * L, L)
        rs = pl.ds(r0, L)
        bi = ci // chunks_per_seq
        a_rows = at_ref[ci]
        for h in range(H):
            hr = pl.ds(h, 1)
            q = qkv_ref[rs, h * HD:(h + 1) * HD]
            k = qkv_ref[rs, DM + h * HD:DM + (h + 1) * HD]
            v = qkv_ref[rs, 2 * DM + h * HD:2 * DM + (h + 1) * HD]
            b_col = bc_ref[h, rs, :]
            a_col = a_ref[h, rs, :]
            C = Co_ref[bi, h]
            n = no_ref[bi, hr, :]
            m_prev = mo_ref[bi, hr, :]

            d_log = jnp.where(causal, b_col[:, 0:L] + a_rows[h:h + 1, :], -jnp.inf)
            inter = b_col + m_prev
            m_t = jnp.maximum(jnp.max(d_log, axis=-1, keepdims=True), inter)
            s = _dot_nt(q, k) * jnp.exp(d_log - m_t[:, 0:L])
            w_inter = jnp.exp(inter - m_t)
            num = _dot(s.astype(BF16), v) + w_inter * _dot(q, C.astype(BF16))
            qn = jnp.sum(q.astype(F32) * n, axis=-1, keepdims=True)
            den = jnp.sum(s, axis=-1, keepdims=True) + w_inter * qn
            hh = num / jnp.maximum(jnp.abs(den), jnp.exp(-m_t))

            m_new = m_t[L - 1:L, :]
            b_last = b_col[L - 1:L, :]
            w_s = jnp.exp(b_last + a_col - m_new)
            w_c = jnp.exp(b_last + m_prev - m_new)
            kw = k.astype(F32) * w_s
            Co_ref[bi, h] = w_c * C + _dot_tn(kw.astype(BF16), v)
            no_ref[bi, hr, :] = w_c * n + jnp.sum(kw, axis=0, keepdims=True)
            mo_ref[bi, hr, :] = m_new

            hn = hh * lax.rsqrt(jnp.mean(hh * hh, axis=-1, keepdims=True) + EPS)
            yb = sig_ref[rs, h * HD:(h + 1) * HD] * (hn * mnw_ref[:, h * HD:(h + 1) * HD])
            y_ref[rs, DC + h * HD:DC + (h + 1) * HD] = yb.astype(BF16)
        return carry

    lax.fori_loop(0, n_chunks, chunk, 0)

    y = _dot(y_ref[...], wout_ref[...])
    xo_ref[...] = (x + _rms(y, gpost_ref[...])).reshape(BB, TT, D)


def _mixer(x, conv0, C0, n0, m0, gpre, w_in, w_g, b_g, conv_w, mnw, w_out, gpost, *, BB, TT, L):
    B, T, D = x.shape
    DC = conv0.shape[-1]
    H, HD = C0.shape[1], C0.shape[-1]
    DM = H * HD
    rows = BB * TT
    seq = lambda s: pl.BlockSpec((BB,) + s, lambda b, t: (b,) + (0,) * len(s))
    xspec = pl.BlockSpec((BB, TT, D), lambda b, t: (b, t, 0))
    kern = functools.partial(_mixer_kernel, BB=BB, TT=TT, L=L, DC=DC, HD=HD)
    return pl.pallas_call(
        kern,
        grid=(B // BB, T // TT),
        in_specs=[xspec, seq((CONV_W - 1, DC)), seq((H, HD, HD)), seq((H, HD)), seq((H, LANES)),
                  _const_spec((1, D)), _const_spec(w_in.shape), _const_spec(w_g.shape),
                  _const_spec(b_g.shape), _const_spec(conv_w.shape), _const_spec((1, DM)),
                  _const_spec(w_out.shape), _const_spec((1, D))],
        out_specs=[xspec, seq((CONV_W - 1, DC)), seq((H, HD, HD)), seq((H, HD)), seq((H, LANES))],
        out_shape=[jax.ShapeDtypeStruct((B, T, D), F32),
                   jax.ShapeDtypeStruct((B, CONV_W - 1, DC), F32),
                   jax.ShapeDtypeStruct((B, H, HD, HD), F32),
                   jax.ShapeDtypeStruct((B, H, HD), F32),
                   jax.ShapeDtypeStruct((B, H, LANES), F32)],
        scratch_shapes=[pltpu.VMEM((BB, TT + CONV_PAD, DC), F32),
                        pltpu.VMEM((rows, 3 * DM), BF16),
                        pltpu.VMEM((rows, DM), F32),
                        pltpu.VMEM((H, rows, LANES), F32),
                        pltpu.VMEM((H, rows, LANES), F32),
                        pltpu.VMEM((rows // L, SUBLANES, L), F32),
                        pltpu.VMEM((rows, DC + DM), BF16)],
        compiler_params=pltpu.CompilerParams(
            dimension_semantics=("arbitrary", "arbitrary"), vmem_limit_bytes=VMEM_LIMIT_BYTES),
        name="token_mixer",
    )(x, conv0, C0, n0, m0, gpre, w_in, w_g, b_g, conv_w, mnw, w_out, gpost)


def _attn_kernel(x_ref, mk_ref, mv_ref, gpre_ref, wq_ref, wo_ref, gpost_ref, xo_ref,
                 q_ref, o_ref, *, BB, TT):
    rows = BB * TT
    D = x_ref.shape[-1]
    HD = D // MEM_HEADS
    x = x_ref[...].reshape(rows, D)
    xn = _rms(x, gpre_ref[...]).astype(BF16)
    q_ref[...] = (_dot(xn, wq_ref[...]) * (HD ** -0.5)).astype(BF16)

    def per_seq(bi, carry):
        rs = pl.ds(pl.multiple_of(bi * TT, TT), TT)
        for h in range(MEM_HEADS):
            hs = slice(h * HD, (h + 1) * HD)
            s = _dot_nt(q_ref[rs, hs], mk_ref[bi, :, hs])
            e = jnp.exp(s - jnp.max(s, axis=-1, keepdims=True))
            p = e / jnp.sum(e, axis=-1, keepdims=True)
            o_ref[rs, hs] = _dot(p.astype(BF16), mv_ref[bi, :, hs]).astype(BF16)
        return carry

    lax.fori_loop(0, BB, per_seq, 0)
    y = _dot(o_ref[...], wo_ref[...])
    xo_ref[...] = (x + _rms(y, gpost_ref[...])).reshape(BB, TT, D)


def _attn(x, mk, mv, gpre, wq, wo, gpost, *, BB, TT):
    B, T, D = x.shape
    M = mk.shape[1]
    xspec = pl.BlockSpec((BB, TT, D), lambda b, t: (b, t, 0))
    mspec = pl.BlockSpec((BB, M, D), lambda b, t: (b, 0, 0))
    kern = functools.partial(_attn_kernel, BB=BB, TT=TT)
    return pl.pallas_call(
        kern,
        grid=(B // BB, T // TT),
        in_specs=[xspec, mspec, mspec, _const_spec((1, D)), _const_spec((D, D)),
                  _const_spec((D, D)), _const_spec((1, D))],
        out_specs=xspec,
        out_shape=jax.ShapeDtypeStruct((B, T, D), F32),
        scratch_shapes=[pltpu.VMEM((BB * TT, D), BF16), pltpu.VMEM((BB * TT, D), BF16)],
        compiler_params=pltpu.CompilerParams(
            dimension_semantics=("arbitrary", "arbitrary"), vmem_limit_bytes=VMEM_LIMIT_BYTES),
        name="mem_attention",
    )(x, mk, mv, gpre, wq, wo, gpost)


def _gelu_tanh(x):
    cdf = 0.5 * (1.0 + jnp.tanh((2.0 / jnp.pi) ** 0.5 * (x + 0.044715 * (x * x * x))))
    return x * cdf


def _ffn_kernel(x_ref, ff0_ref, gpre_ref, wup_ref, cw_ref, wdn_ref, gpost_ref,
                xo_ref, ffo_ref, ext_ref, acc_ref, *, BB, TT, FC):
    rows = BB * TT
    D = x_ref.shape[-1]
    DFF = wdn_ref.shape[0]

    @pl.when(pl.program_id(1) == 0)
    def _():
        ffo_ref[...] = ff0_ref[...]

    x = x_ref[...].reshape(rows, D)
    xn = _rms(x, gpre_ref[...]).astype(BF16)

    def conv_cols(c0):
        cols = slice(c0, c0 + FC)
        u = _dot(xn, wup_ref[:, cols]).reshape(BB, TT, FC)
        y, tail = _causal_conv(ext_ref, ffo_ref[:, :, cols], u, cw_ref, cols, TT)
        ffo_ref[:, :, cols] = tail
        return y.reshape(rows, FC)

    for j in range(DFF // FC):
        a = conv_cols(j * FC)
        gate = conv_cols(DFF + j * FC)
        act = (_gelu_tanh(gate) * a).astype(BF16)
        part = _dot(act, wdn_ref[j * FC:(j + 1) * FC, :])
        if j == 0:
            acc_ref[...] = part
        else:
            acc_ref[...] += part

    xo_ref[...] = (x + _rms(acc_ref[...], gpost_ref[...])).reshape(BB, TT, D)


def _ffn(x, ff0, gpre, w_up, conv_w, w_down, gpost, *, BB, TT, FC):
    B, T, D = x.shape
    DFF2 = w_up.shape[1]
    xspec = pl.BlockSpec((BB, TT, D), lambda b, t: (b, t, 0))
    sspec = pl.BlockSpec((BB, CONV_W - 1, DFF2), lambda b, t: (b, 0, 0))
    kern = functools.partial(_ffn_kernel, BB=BB, TT=TT, FC=FC)
    return pl.pallas_call(
        kern,
        grid=(B // BB, T // TT),
        in_specs=[xspec, sspec, _const_spec((1, D)), _const_spec(w_up.shape),
                  _const_spec(conv_w.shape), _const_spec(w_down.shape), _const_spec((1, D))],
        out_specs=[xspec, sspec],
        out_shape=[jax.ShapeDtypeStruct((B, T, D), F32),
                   jax.ShapeDtypeStruct((B, CONV_W - 1, DFF2), F32)],
        scratch_shapes=[pltpu.VMEM((BB, TT + CONV_PAD, FC), F32),
                        pltpu.VMEM((BB * TT, D), F32)],
        compiler_params=pltpu.CompilerParams(
            dimension_semantics=("arbitrary", "arbitrary"), vmem_limit_bytes=VMEM_LIMIT_BYTES),
        name="conv_ffn",
    )(x, ff0, gpre, w_up, conv_w, w_down, gpost)


def _tiles(T):
    TT = min(T, 512)
    return TT, min(TT, LANES)


def _layer(x, mk, mv, conv0, C0, n0, m0, ff0, p, *, BB):
    B, T, D = x.shape
    TT, L = _tiles(T)
    H = C0.shape[1]
    m0_rep = jnp.broadcast_to(m0[:, :, None], (B, H, LANES))
    x, conv_new, C, n, m_rep = _mixer(x, conv0, C0, n0, m0_rep, p['g_mix_pre'], p['w_in'], p['w_g'], p['b_g'],
                                      p['conv_w'], p['mnw'], p['w_out'], p['g_mix_post'], BB=BB, TT=TT, L=L)
    x = _attn(x, mk.reshape(B, -1, D).astype(BF16), mv.reshape(B, -1, D).astype(BF16),
              p['g_mem_pre'], p['w_mq'], p['w_mo'], p['g_mem_post'], BB=BB, TT=TT)
    x, ff_new = _ffn(x, ff0, p['g_ffn_pre'], p['w_up'], p['ffn_conv_w'], p['w_down'], p['g_ffn_post'],
                     BB=BB, TT=TT, FC=256)
    return x, conv_new, C, n, m_rep[:, :, 0], ff_new


def kernel(x_prompt, x_sample, mem_prompt, cache_mem_k, cache_mem_v, state_conv, state_mlstm_C, state_mlstm_n, state_mlstm_m, state_ffn_conv, norm_mix_pre, w_in, b_gates, conv_w, mlstm_norm_w, w_out, norm_mix_post, norm_mem_pre, norm_mem_kv, w_mq, w_mk, w_mv, w_mo, norm_mem_post, norm_ffn_pre, w_up, ffn_conv_w, w_down, norm_ffn_post):
    depth = w_in.shape[0]
    B, _, D = x_prompt.shape
    BS = x_sample.shape[0]
    DC = state_conv.shape[-1]
    H, HD = state_mlstm_C.shape[2], state_mlstm_C.shape[-1]
    DM = H * HD
    DFF2 = state_ffn_conv.shape[-1]
    n_main = 3 * DC + 4 * DM
    n_mem = mem_prompt.shape[1]

    xp, xs = x_prompt, x_sample
    outs = [[] for _ in range(12)]
    for l in range(depth):
        row = lambda a: a[l][None, :]
        w_g = jnp.zeros((D, 2 * LANES), F32)
        w_g = w_g.at[:, 0:H].set(w_in[l][:, n_main:n_main + H])
        w_g = w_g.at[:, LANES:LANES + H].set(w_in[l][:, n_main + H:n_main + 2 * H])
        b_g = jnp.zeros((1, 2 * LANES), F32)
        b_g = b_g.at[0, 0:H].set(b_gates[l][0:H]).at[0, LANES:LANES + H].set(b_gates[l][H:2 * H])
        p = {'g_mix_pre': row(norm_mix_pre), 'w_in': w_in[l][:, 0:n_main].astype(BF16),
             'w_g': w_g.astype(BF16), 'b_g': b_g, 'conv_w': conv_w[l], 'mnw': row(mlstm_norm_w),
             'w_out': w_out[l].astype(BF16), 'g_mix_post': row(norm_mix_post),
             'g_mem_pre': row(norm_mem_pre), 'w_mq': w_mq[l].astype(BF16), 'w_mo': w_mo[l].astype(BF16),
             'g_mem_post': row(norm_mem_post), 'g_ffn_pre': row(norm_ffn_pre),
             'w_up': w_up[l].astype(BF16), 'ffn_conv_w': ffn_conv_w[l], 'w_down': w_down[l].astype(BF16),
             'g_ffn_post': row(norm_ffn_post)}

        mk_p, mv_p = _memkv(mem_prompt, row(norm_mem_kv), w_mk[l].astype(BF16), w_mv[l].astype(BF16))
        xp, conv_p, C_p, n_p, m_p, ff_p = _layer(
            xp, mk_p, mv_p,
            jnp.zeros((B, CONV_W - 1, DC), F32), jnp.zeros((B, H, HD, HD), F32),
            jnp.zeros((B, H, HD), F32), jnp.zeros((B, H), F32),
            jnp.zeros((B, CONV_W - 1, DFF2), F32), p, BB=1)
        xs, conv_s, C_s, n_s, m_s, ff_s = _layer(
            xs, cache_mem_k[l], cache_mem_v[l], state_conv[l], state_mlstm_C[l], state_mlstm_n[l],
            state_mlstm_m[l], state_ffn_conv[l], p, BB=8)
        heads = (B, n_mem, MEM_HEADS, D // MEM_HEADS)
        for lst, val in zip(outs, (mk_p.reshape(heads), mv_p.reshape(heads), conv_p, conv_s, C_p, C_s,
                                   n_p, n_s, m_p, m_s, ff_p, ff_s)):
            lst.append(val)
    return (xp, xs) + tuple(jnp.stack(o) for o in outs)
```

```python
import functools

import jax
import jax.numpy as jnp
from jax import lax
from jax.experimental import pallas as pl
from jax.experimental.pallas import tpu as pltpu

F32 = jnp.float32
BF16 = jnp.bfloat16

EPS = 1e-6
CONV_W = 3
MLSTM_HEADS = 4
MEM_HEADS = 4
LANES = 128
SUBLANES = 8
CONV_PAD = SUBLANES
FFN_CHUNK = 256
VMEM_LIMIT_BYTES = 56 * 1024 * 1024


def _rms(x, g):
    return x * lax.rsqrt(jnp.mean(x * x, axis=-1, keepdims=True) + EPS) * g


def _dot(a, b):
    return jnp.dot(a, b, preferred_element_type=F32)


def _dot_nt(a, b):
    return lax.dot_general(a, b, (((1,), (1,)), ((), ())), preferred_element_type=F32)


def _dot_tn(a, b):
    return lax.dot_general(a, b, (((0,), (0,)), ((), ())), preferred_element_type=F32)


def _const_spec(shape):
    zeros = (0,) * len(shape)
    return pl.BlockSpec(shape, lambda b, t: zeros, pipeline_mode=pl.Buffered(1))


def _causal_conv(ext_ref, tail, u, w_ref, cols, TT):
    lo = CONV_PAD - (CONV_W - 1)
    ext_ref[:, lo:CONV_PAD, :] = tail
    ext_ref[:, CONV_PAD:CONV_PAD + TT, :] = u
    y = ext_ref[:, lo:lo + TT, :] * w_ref[0:1, cols]
    for j in range(1, CONV_W):
        y = y + ext_ref[:, lo + j:lo + j + TT, :] * w_ref[j:j + 1, cols]
    return y, ext_ref[:, TT + lo:TT + CONV_PAD, :]


def _memkv_kernel(mem_ref, g_ref, wk_ref, wv_ref, k_ref, v_ref):
    mn = _rms(mem_ref[0], g_ref[...]).astype(BF16)
    k_ref[0] = _dot(mn, wk_ref[...])
    v_ref[0] = _dot(mn, wv_ref[...])


def _memkv(mem, g, wk, wv):
    B, M, D = mem.shape
    spec = pl.BlockSpec((1, M, D), lambda b: (b, 0, 0))
    full = lambda s: pl.BlockSpec(s, lambda b: (0,) * len(s))
    return pl.pallas_call(
        _memkv_kernel,
        grid=(B,),
        in_specs=[spec, full((1, D)), full((D, D)), full((D, D))],
        out_specs=[spec, spec],
        out_shape=[jax.ShapeDtypeStruct((B, M, D), F32)] * 2,
        compiler_params=pltpu.CompilerParams(
            dimension_semantics=("arbitrary",), vmem_limit_bytes=VMEM_LIMIT_BYTES),
        name="mem_kv",
    )(mem, g, wk, wv)


def _mixer_kernel(x_ref, conv0_ref, C0_ref, n0_ref, m0_ref, gpre_ref, win_ref, wg_ref, bg_ref,
                  convw_ref, mnw_ref, wout_ref, gpost_ref,
                  xo_ref, convo_ref, Co_ref, no_ref, mo_ref,
                  ext_ref, qkv_ref, sig_ref, bc_ref, a_ref, at_ref, y_ref,
                  *, BB, TT, L, DC, HD):
    rows = BB * TT
    H = MLSTM_HEADS
    DM = H * HD
    D = x_ref.shape[-1]
    n_chunks = rows // L
    chunks_per_seq = TT // L

    @pl.when(pl.program_id(1) == 0)
    def _():
        convo_ref[...] = conv0_ref[...]
        Co_ref[...] = C0_ref[...]
        no_ref[...] = n0_ref[...]
        mo_ref[...] = m0_ref[...]

    x = x_ref[...].reshape(rows, D)
    xn = _rms(x, gpre_ref[...]).astype(BF16)

    pa = _dot(xn, win_ref[:, 0:3 * DC])
    u = (pa[:, 2 * DC:3 * DC] * pa[:, 0:DC]).reshape(BB, TT, DC)
    ya, tail = _causal_conv(ext_ref, convo_ref[...], u, convw_ref, slice(0, DC), TT)
    convo_ref[...] = tail
    y_ref[:, 0:DC] = (pa[:, DC:2 * DC] * ya.reshape(rows, DC)).astype(BF16)

    pm = _dot(xn, win_ref[:, 3 * DC:3 * DC + 4 * DM])
    qkv_ref[:, 0:DM] = pm[:, 0:DM].astype(BF16)
    qkv_ref[:, DM:2 * DM] = (pm[:, DM:2 * DM] * (HD ** -0.5)).astype(BF16)
    qkv_ref[:, 2 * DM:3 * DM] = pm[:, 2 * DM:3 * DM].astype(BF16)
    sig_ref[...] = 1.0 / (1.0 + jnp.exp(-pm[:, 3 * DM:4 * DM]))

    g = _dot(xn, wg_ref[...]) + bg_ref[...]
    ipre = g[:, 0:LANES]
    fpre = g[:, LANES:2 * LANES]
    logf = jnp.minimum(fpre, 0.0) - jnp.log1p(jnp.exp(-jnp.abs(fpre)))
    pos = lax.broadcasted_iota(jnp.int32, (rows, LANES), 0) & (L - 1)
    bc = logf
    sh = 1
    while sh < L:
        bc = bc + jnp.where(pos >= sh, pltpu.roll(bc, sh, axis=0), 0.0)
        sh *= 2
    a = ipre - bc
    a_t = a.T
    for c in range(n_chunks):
        at_ref[c] = a_t[0:SUBLANES, c * L:(c + 1) * L]
    for h in range(H):
        bc_ref[h] = jnp.broadcast_to(bc[:, h:h + 1], (rows, LANES))
        a_ref[h] = jnp.broadcast_to(a[:, h:h + 1], (rows, LANES))

    causal = (lax.broadcasted_iota(jnp.int32, (L, L), 0)
              >= lax.broadcasted_iota(jnp.int32, (L, L), 1))

    for ci in range(n_chunks):
        rs = pl.ds(ci * L, L)
        bi = ci // chunks_per_seq
        a_rows = at_ref[ci]
        for h in range(H):
            hr = pl.ds(h, 1)
            q = qkv_ref[rs, h * HD:(h + 1) * HD]
            k = qkv_ref[rs, DM + h * HD:DM + (h + 1) * HD]
            v = qkv_ref[rs, 2 * DM + h * HD:2 * DM + (h + 1) * HD]
            b_col = bc_ref[h, rs, :]
            a_col = a_ref[h, rs, :]
            C = Co_ref[bi, h]
            n = no_ref[bi, hr, :]
            m_prev = mo_ref[bi, hr, :]

            d_log = jnp.where(causal, b_col[:, 0:L] + a_rows[h:h + 1, :], -jnp.inf)
            inter = b_col + m_prev
            m_t = jnp.maximum(jnp.max(d_log, axis=-1, keepdims=True), inter)
            s = _dot_nt(q, k) * jnp.exp(d_log - m_t[:, 0:L])
            w_inter = jnp.exp(inter - m_t)
            num = _dot(s.astype(BF16), v) + w_inter * _dot(q, C.astype(BF16))
            qn = jnp.sum(q.astype(F32) * n, axis=-1, keepdims=True)
            den = jnp.sum(s, axis=-1, keepdims=True) + w_inter * qn
            hh = num / jnp.maximum(jnp.abs(den), jnp.exp(-m_t))

            m_new = m_t[L - 1:L, :]
            b_last = b_col[L - 1:L, :]
            w_s = jnp.exp(b_last + a_col - m_new)
            w_c = jnp.exp(b_last + m_prev - m_new)
            kw = k.astype(F32) * w_s
            Co_ref[bi, h] = w_c * C + _dot_tn(kw.astype(BF16), v)
            no_ref[bi, hr, :] = w_c * n + jnp.sum(kw, axis=0, keepdims=True)
            mo_ref[bi, hr, :] = m_new

            hn = hh * lax.rsqrt(jnp.mean(hh * hh, axis=-1, keepdims=True) + EPS)
            yb = sig_ref[rs, h * HD:(h + 1) * HD] * (hn * mnw_ref[:, h * HD:(h + 1) * HD])
            y_ref[rs, DC + h * HD:DC + (h + 1) * HD] = yb.astype(BF16)

    y = _dot(y_ref[...], wout_ref[...])
    xo_ref[...] = (x + _rms(y, gpost_ref[...])).reshape(BB, TT, D)


def _mixer(x, conv0, C0, n0, m0, gpre, w_in, w_g, b_g, conv_w, mnw, w_out, gpost, *, BB, TT, L):
    B, T, D = x.shape
    DC = conv0.shape[-1]
    H, HD = C0.shape[1], C0.shape[-1]
    DM = H * HD
    rows = BB * TT
    seq = lambda s: pl.BlockSpec((BB,) + s, lambda b, t: (b,) + (0,) * len(s))
    xspec = pl.BlockSpec((BB, TT, D), lambda b, t: (b, t, 0))
    kern = functools.partial(_mixer_kernel, BB=BB, TT=TT, L=L, DC=DC, HD=HD)
    return pl.pallas_call(
        kern,
        grid=(B // BB, T // TT),
        in_specs=[xspec, seq((CONV_W - 1, DC)), seq((H, HD, HD)), seq((H, HD)), seq((H, LANES)),
                  _const_spec((1, D)), _const_spec(w_in.shape), _const_spec(w_g.shape),
                  _const_spec(b_g.shape), _const_spec(conv_w.shape), _const_spec((1, DM)),
                  _const_spec(w_out.shape), _const_spec((1, D))],
        out_specs=[xspec, seq((CONV_W - 1, DC)), seq((H, HD, HD)), seq((H, HD)), seq((H, LANES))],
        out_shape=[jax.ShapeDtypeStruct((B, T, D), F32),
                   jax.ShapeDtypeStruct((B, CONV_W - 1, DC), F32),
                   jax.ShapeDtypeStruct((B, H, HD, HD), F32),
                   jax.ShapeDtypeStruct((B, H, HD), F32),
                   jax.ShapeDtypeStruct((B, H, LANES), F32)],
        scratch_shapes=[pltpu.VMEM((BB, TT + CONV_PAD, DC), F32),
                        pltpu.VMEM((rows, 3 * DM), BF16),
                        pltpu.VMEM((rows, DM), F32),
                        pltpu.VMEM((H, rows, LANES), F32),
                        pltpu.VMEM((H, rows, LANES), F32),
                        pltpu.VMEM((rows // L, SUBLANES, L), F32),
                        pltpu.VMEM((rows, DC + DM), BF16)],
        compiler_params=pltpu.CompilerParams(
            dimension_semantics=("arbitrary", "arbitrary"), vmem_limit_bytes=VMEM_LIMIT_BYTES),
        name="token_mixer",
    )(x, conv0, C0, n0, m0, gpre, w_in, w_g, b_g, conv_w, mnw, w_out, gpost)


def _attn_kernel(x_ref, mk_ref, mv_ref, gpre_ref, wq_ref, wo_ref, gpost_ref, xo_ref,
                 q_ref, o_ref, *, BB, TT):
    rows = BB * TT
    D = x_ref.shape[-1]
    HD = D // MEM_HEADS
    x = x_ref[...].reshape(rows, D)
    xn = _rms(x, gpre_ref[...]).astype(BF16)
    q_ref[...] = (_dot(xn, wq_ref[...]) * (HD ** -0.5)).astype(BF16)

    for bi in range(BB):
        rs = pl.ds(bi * TT, TT)
        for h in range(MEM_HEADS):
            hs = slice(h * HD, (h + 1) * HD)
            s = _dot_nt(q_ref[rs, hs], mk_ref[bi, :, hs])
            e = jnp.exp(s - jnp.max(s, axis=-1, keepdims=True))
            p = e / jnp.sum(e, axis=-1, keepdims=True)
            o_ref[rs, hs] = _dot(p.astype(BF16), mv_ref[bi, :, hs]).astype(BF16)

    y = _dot(o_ref[...], wo_ref[...])
    xo_ref[...] = (x + _rms(y, gpost_ref[...])).reshape(BB, TT, D)


def _attn(x, mk, mv, gpre, wq, wo, gpost, *, BB, TT):
    B, T, D = x.shape
    M = mk.shape[1]
    xspec = pl.BlockSpec((BB, TT, D), lambda b, t: (b, t, 0))
    mspec = pl.BlockSpec((BB, M, D), lambda b, t: (b, 0, 0))
    kern = functools.partial(_attn_kernel, BB=BB, TT=TT)
    return pl.pallas_call(
        kern,
        grid=(B // BB, T // TT),
        in_specs=[xspec, mspec, mspec, _const_spec((1, D)), _const_spec((D, D)),
                  _const_spec((D, D)), _const_spec((1, D))],
        out_specs=xspec,
        out_shape=jax.ShapeDtypeStruct((B, T, D), F32),
        scratch_shapes=[pltpu.VMEM((BB * TT, D), BF16), pltpu.VMEM((BB * TT, D), BF16)],
        compiler_params=pltpu.CompilerParams(
            dimension_semantics=("arbitrary", "arbitrary"), vmem_limit_bytes=VMEM_LIMIT_BYTES),
        name="mem_attention",
    )(x, mk, mv, gpre, wq, wo, gpost)


def _ffn_kernel(x_ref, ff0_ref, gpre_ref, wup_ref, cw_ref, wdn_ref, gpost_ref,
                xo_ref, ffo_ref, xn_ref, ext_ref, act_ref, acc_ref, *, BB, TT, RB):
    rows = BB * TT
    D = x_ref.shape[-1]
    n_chunks, FC = wdn_ref.shape[0], wdn_ref.shape[1]
    lo = CONV_PAD - (CONV_W - 1)

    @pl.when(pl.program_id(1) == 0)
    def _():
        ffo_ref[...] = ff0_ref[...]

    def up(j, p):
        ext_ref[p, :, lo:CONV_PAD, :] = ffo_ref[j]
        ext_ref[p, :, CONV_PAD:CONV_PAD + TT, :] = _dot(xn_ref[...], wup_ref[j]).reshape(BB, TT, 2 * FC)
        ffo_ref[j] = ext_ref[p, :, TT + lo:TT + CONV_PAD, :]

    def conv(j, p, cols, bi, r0):
        y = ext_ref[p, bi, lo + r0:lo + r0 + RB, cols] * cw_ref[j, 0:1, cols]
        for k in range(1, CONV_W):
            y = y + ext_ref[p, bi, lo + k + r0:lo + k + r0 + RB, cols] * cw_ref[j, k:k + 1, cols]
        return y

    def gate(j, p):
        for bi in range(BB):
            for r0 in range(0, TT, RB):
                a = conv(j, p, slice(0, FC), bi, r0)
                g = conv(j, p, slice(FC, 2 * FC), bi, r0)
                t = jnp.tanh((2.0 / jnp.pi) ** 0.5 * (g + 0.044715 * (g * g * g)))
                act_ref[p, bi * TT + r0:bi * TT + r0 + RB, :] = ((g * a) * (1.0 + t)).astype(BF16)

    def down(j, p):
        acc_ref[...] += _dot(act_ref[p], wdn_ref[j])

    def stage(j, p):
        up(j + 1, 1 - p)
        gate(j, p)
        down(j - 1, 1 - p)

    xn_ref[...] = _rms(x_ref[...].reshape(rows, D), gpre_ref[...]).astype(BF16)
    acc_ref[...] = jnp.zeros_like(acc_ref)
    up(0, 0)
    up(1, 1)
    gate(0, 0)

    for j in range(1, n_chunks - 1):
        stage(j, j % 2)
    last = n_chunks - 1
    gate(last, last % 2)
    down(last - 1, (last - 1) % 2)
    down(last, last % 2)

    x = x_ref[...].reshape(rows, D)
    xo_ref[...] = (x + _rms(acc_ref[...], gpost_ref[...])).reshape(BB, TT, D)


def _ffn(x, ff0, gpre, w_up, conv_w, w_down, gpost, *, BB, TT, RB):
    B, T, D = x.shape
    n_chunks, FC, _ = w_down.shape
    xspec = pl.BlockSpec((BB, TT, D), lambda b, t: (b, t, 0))
    sspec = pl.BlockSpec((n_chunks, BB, CONV_W - 1, 2 * FC), lambda b, t: (0, b, 0, 0))
    kern = functools.partial(_ffn_kernel, BB=BB, TT=TT, RB=RB)
    return pl.pallas_call(
        kern,
        grid=(B // BB, T // TT),
        in_specs=[xspec, sspec, _const_spec((1, D)), _const_spec(w_up.shape),
                  _const_spec(conv_w.shape), _const_spec(w_down.shape), _const_spec((1, D))],
        out_specs=[xspec, sspec],
        out_shape=[jax.ShapeDtypeStruct((B, T, D), F32),
                   jax.ShapeDtypeStruct(ff0.shape, F32)],
        scratch_shapes=[pltpu.VMEM((BB * TT, D), BF16),
                        pltpu.VMEM((2, BB, TT + CONV_PAD, 2 * FC), F32),
                        pltpu.VMEM((2, BB * TT, FC), BF16),
                        pltpu.VMEM((BB * TT, D), F32)],
        compiler_params=pltpu.CompilerParams(
            dimension_semantics=("arbitrary", "arbitrary"), vmem_limit_bytes=VMEM_LIMIT_BYTES),
        name="conv_ffn",
    )(x, ff0, gpre, w_up, conv_w, w_down, gpost)


def _chunk_cols(w, n_chunks):
    lead = w.shape[:-1]
    w = w.reshape(lead + (2, n_chunks, -1))
    w = jnp.moveaxis(w, -2, 0)
    return w.reshape((n_chunks,) + lead + (-1,))


def _unchunk_cols(w):
    n_chunks, lead = w.shape[0], w.shape[1:-1]
    w = w.reshape((n_chunks,) + lead + (2, -1))
    w = jnp.moveaxis(w, 0, -2)
    return w.reshape(lead + (-1,))


def _tiles(T):
    TT = min(T, 512)
    return TT, min(TT, LANES)


def _layer(x, mk, mv, conv0, C0, n0, m0, ff0, p, *, BB):
    B, T, D = x.shape
    TT, L = _tiles(T)
    H = C0.shape[1]
    m0_rep = jnp.broadcast_to(m0[:, :, None], (B, H, LANES))
    x, conv_new, C, n, m_rep = _mixer(x, conv0, C0, n0, m0_rep, p['g_mix_pre'], p['w_in'], p['w_g'], p['b_g'],
                                      p['conv_w'], p['mnw'], p['w_out'], p['g_mix_post'], BB=BB, TT=TT, L=L)
    x = _attn(x, mk.reshape(B, -1, D).astype(BF16), mv.reshape(B, -1, D).astype(BF16),
              p['g_mem_pre'], p['w_mq'], p['w_mo'], p['g_mem_post'], BB=BB, TT=TT)
    n_ff = p['w_down_half'].shape[0]
    x, ff_new = _ffn(x, _chunk_cols(ff0, n_ff), p['g_ffn_pre'], p['w_up'], p['ffn_conv_w'], p['w_down_half'],
                     p['g_ffn_post'], BB=BB, TT=TT, RB=min(TT, 64))
    return x, conv_new, C, n, m_rep[:, :, 0], _unchunk_cols(ff_new)


def kernel(x_prompt, x_sample, mem_prompt, cache_mem_k, cache_mem_v, state_conv, state_mlstm_C, state_mlstm_n, state_mlstm_m, state_ffn_conv, norm_mix_pre, w_in, b_gates, conv_w, mlstm_norm_w, w_out, norm_mix_post, norm_mem_pre, norm_mem_kv, w_mq, w_mk, w_mv, w_mo, norm_mem_post, norm_ffn_pre, w_up, ffn_conv_w, w_down, norm_ffn_post):
    depth = w_in.shape[0]
    B, _, D = x_prompt.shape
    DC = state_conv.shape[-1]
    H, HD = state_mlstm_C.shape[2], state_mlstm_C.shape[-1]
    DM = H * HD
    DFF2 = state_ffn_conv.shape[-1]
    n_main = 3 * DC + 4 * DM
    n_mem = mem_prompt.shape[1]
    n_ff = w_down.shape[1] // FFN_CHUNK

    xp, xs = x_prompt, x_sample
    outs = [[] for _ in range(12)]
    for l in range(depth):
        row = lambda a: a[l][None, :]
        w_g = jnp.zeros((D, 2 * LANES), F32)
        w_g = w_g.at[:, 0:H].set(w_in[l][:, n_main:n_main + H])
        w_g = w_g.at[:, LANES:LANES + H].set(w_in[l][:, n_main + H:n_main + 2 * H])
        b_g = jnp.zeros((1, 2 * LANES), F32)
        b_g = b_g.at[0, 0:H].set(b_gates[l][0:H]).at[0, LANES:LANES + H].set(b_gates[l][H:2 * H])
        p = {'g_mix_pre': row(norm_mix_pre), 'w_in': w_in[l][:, 0:n_main].astype(BF16),
             'w_g': w_g.astype(BF16), 'b_g': b_g, 'conv_w': conv_w[l], 'mnw': row(mlstm_norm_w),
             'w_out': w_out[l].astype(BF16), 'g_mix_post': row(norm_mix_post),
             'g_mem_pre': row(norm_mem_pre), 'w_mq': w_mq[l].astype(BF16), 'w_mo': w_mo[l].astype(BF16),
             'g_mem_post': row(norm_mem_post), 'g_ffn_pre': row(norm_ffn_pre),
             'w_up': _chunk_cols(w_up[l].astype(BF16), n_ff), 'ffn_conv_w': _chunk_cols(ffn_conv_w[l], n_ff),
             'w_down_half': (0.5 * w_down[l]).astype(BF16).reshape(n_ff, FFN_CHUNK, D),
             'g_ffn_post': row(norm_ffn_post)}

        mk_p, mv_p = _memkv(mem_prompt, row(norm_mem_kv), w_mk[l].astype(BF16), w_mv[l].astype(BF16))
        xp, conv_p, C_p, n_p, m_p, ff_p = _layer(
            xp, mk_p, mv_p,
            jnp.zeros((B, CONV_W - 1, DC), F32), jnp.zeros((B, H, HD, HD), F32),
            jnp.zeros((B, H, HD), F32), jnp.zeros((B, H), F32),
            jnp.zeros((B, CONV_W - 1, DFF2), F32), p, BB=1)
        xs, conv_s, C_s, n_s, m_s, ff_s = _layer(
            xs, cache_mem_k[l], cache_mem_v[l], state_conv[l], state_mlstm_C[l], state_mlstm_n[l],
            state_mlstm_m[l], state_ffn_conv[l], p, BB=8)
        heads = (B, n_mem, MEM_HEADS, D // MEM_HEADS)
        for lst, val in zip(outs, (mk_p.reshape(heads), mv_p.reshape(heads), conv_p, conv_s, C_p, C_s,
                                   n_p, n_s, m_p, m_s, ff_p, ff_s)):
            lst.append(val)
    return (xp, xs) + tuple(jnp.stack(o) for o in outs)
```

```python
import functools

import jax
import jax.numpy as jnp
from jax import lax
from jax.experimental import pallas as pl
from jax.experimental.pallas import tpu as pltpu

F32 = jnp.float32
BF16 = jnp.bfloat16

EPS = 1e-6
CONV_W = 3
MLSTM_HEADS = 4
MEM_HEADS = 4
LANES = 128
SUBLANES = 8
CONV_PAD = SUBLANES
FFN_CHUNK = 256
VMEM_LIMIT_BYTES = 56 * 1024 * 1024


def _rms(x, g):
    return x * lax.rsqrt(jnp.mean(x * x, axis=-1, keepdims=True) + EPS) * g


def _dot(a, b):
    return jnp.dot(a, b, preferred_element_type=F32)


def _dot_nt(a, b):
    return lax.dot_general(a, b, (((1,), (1,)), ((), ())), preferred_element_type=F32)


def _const_spec(shape):
    zeros = (0,) * len(shape)
    return pl.BlockSpec(shape, lambda b, t: zeros, pipeline_mode=pl.Buffered(1))


def _causal_conv(ext_ref, tail, u, w_ref, cols, TT):
    lo = CONV_PAD - (CONV_W - 1)
    ext_ref[:, lo:CONV_PAD, :] = tail
    ext_ref[:, CONV_PAD:CONV_PAD + TT, :] = u
    y = ext_ref[:, lo:lo + TT, :] * w_ref[0:1, cols]
    for j in range(1, CONV_W):
        y = y + ext_ref[:, lo + j:lo + j + TT, :] * w_ref[j:j + 1, cols]
    return y, ext_ref[:, TT + lo:TT + CONV_PAD, :]


def _memkv_kernel(mem_ref, g_ref, wk_ref, wv_ref, k_ref, v_ref):
    mn = _rms(mem_ref[0], g_ref[...]).astype(BF16)
    k_ref[0] = _dot(mn, wk_ref[...])
    v_ref[0] = _dot(mn, wv_ref[...])


def _memkv(mem, g, wk, wv):
    B, M, D = mem.shape
    spec = pl.BlockSpec((1, M, D), lambda b: (b, 0, 0))
    full = lambda s: pl.BlockSpec(s, lambda b: (0,) * len(s))
    return pl.pallas_call(
        _memkv_kernel,
        grid=(B,),
        in_specs=[spec, full((1, D)), full((D, D)), full((D, D))],
        out_specs=[spec, spec],
        out_shape=[jax.ShapeDtypeStruct((B, M, D), F32)] * 2,
        compiler_params=pltpu.CompilerParams(
            dimension_semantics=("arbitrary",), vmem_limit_bytes=VMEM_LIMIT_BYTES),
        name="mem_kv",
    )(mem, g, wk, wv)


def _mixer_kernel(x_ref, conv0_ref, C0_ref, m0_ref, gpre_ref, win_ref, wkt_ref, wg_ref, bg_ref,
                  convw_ref, mnw_ref, wout_ref, gpost_ref,
                  xo_ref, convo_ref, Co_ref, mo_ref,
                  ext_ref, q_ref, kt_ref, vaug_ref, sig_ref, bc_ref, cm_ref, at_ref, y_ref,
                  *, BB, TT, L, DC, HD):
    rows = BB * TT
    H = MLSTM_HEADS
    DM = H * HD
    D = x_ref.shape[-1]
    n_chunks = rows // L
    chunks_per_seq = TT // L

    @pl.when(pl.program_id(1) == 0)
    def _():
        convo_ref[...] = conv0_ref[...]
        Co_ref[...] = C0_ref[...]
        mo_ref[...] = m0_ref[...]

    x = x_ref[...].reshape(rows, D)
    xn = _rms(x, gpre_ref[...]).astype(BF16)

    pa = _dot(xn, win_ref[:, 0:3 * DC])
    u = (pa[:, 2 * DC:3 * DC] * pa[:, 0:DC]).reshape(BB, TT, DC)
    ya, tail = _causal_conv(ext_ref, convo_ref[...], u, convw_ref, slice(0, DC), TT)
    convo_ref[...] = tail
    y_ref[:, 0:DC] = (pa[:, DC:2 * DC] * ya.reshape(rows, DC)).astype(BF16)

    c0 = 3 * DC
    q_ref[...] = _dot(xn, win_ref[:, c0:c0 + DM]).astype(BF16)
    kt_ref[...] = (_dot_nt(wkt_ref[...], xn) * (HD ** -0.5)).astype(BF16)
    vo = _dot(xn, win_ref[:, c0 + 2 * DM:c0 + 4 * DM])
    for h in range(H):
        vaug_ref[h, :, 0:HD] = vo[:, h * HD:(h + 1) * HD].astype(BF16)
        vaug_ref[h, :, HD:2 * HD] = jnp.ones((rows, HD), BF16)
    sig_ref[...] = 1.0 / (1.0 + jnp.exp(-vo[:, DM:2 * DM]))

    g = _dot(xn, wg_ref[...]) + bg_ref[...]
    ipre = g[:, 0:LANES]
    fpre = g[:, LANES:2 * LANES]
    logf = jnp.minimum(fpre, 0.0) - jnp.log1p(jnp.exp(-jnp.abs(fpre)))
    pos = lax.broadcasted_iota(jnp.int32, (rows, LANES), 0) & (L - 1)
    bc = logf
    sh = 1
    while sh < L:
        bc = bc + jnp.where(pos >= sh, pltpu.roll(bc, sh, axis=0), 0.0)
        sh *= 2
    a = ipre - bc
    cm = a
    sh = 1
    while sh < L:
        cm = jnp.maximum(cm, jnp.where(pos >= sh, pltpu.roll(cm, sh, axis=0), -jnp.inf))
        sh *= 2
    a_t = a.T
    for c in range(n_chunks):
        at_ref[c] = a_t[0:SUBLANES, c * L:(c + 1) * L]
    for h in range(H):
        bc_ref[h] = jnp.broadcast_to(bc[:, h:h + 1], (rows, LANES))
        cm_ref[h] = jnp.broadcast_to(cm[:, h:h + 1], (rows, LANES))

    causal = (lax.broadcasted_iota(jnp.int32, (L, L), 0)
              >= lax.broadcasted_iota(jnp.int32, (L, L), 1))

    for ci in range(n_chunks):
        rs = pl.ds(ci * L, L)
        bi = ci // chunks_per_seq
        a_rows = at_ref[ci]
        for h in range(H):
            hr = pl.ds(h, 1)
            q = q_ref[rs, h * HD:(h + 1) * HD]
            kt = kt_ref[h * HD:(h + 1) * HD, ci * L:(ci + 1) * L]
            vaug = vaug_ref[h, rs, :]
            b_col = bc_ref[h, rs, :]
            a_row = a_rows[h:h + 1, :]
            Caug = Co_ref[bi, h]
            m_prev = mo_ref[bi, hr, :]

            m_t = b_col + jnp.maximum(cm_ref[h, rs, :], m_prev)
            d_log = jnp.where(causal, b_col[:, 0:L] + a_row, -jnp.inf)
            s = _dot(q, kt) * jnp.exp(d_log - m_t[:, 0:L])
            w_inter = jnp.exp(b_col + m_prev - m_t)
            intra = _dot(s.astype(BF16), vaug)
            inter = _dot(q, Caug.astype(BF16))
            num = intra[:, 0:HD] + w_inter * inter[:, 0:HD]
            den = intra[:, HD:2 * HD] + w_inter * inter[:, HD:2 * HD]
            hh = num / jnp.maximum(jnp.abs(den), jnp.exp(-m_t))

            m_new = m_t[L - 1:L, :]
            b_last = b_col[L - 1:L, :]
            w_s = jnp.exp(b_last[:, 0:L] + a_row - m_new[:, 0:L])
            w_c = jnp.exp(b_last + m_prev - m_new)
            kw = (kt.astype(F32) * w_s).astype(BF16)
            upd = _dot(kw, vaug)
            Co_ref[bi, h, :, 0:HD] = w_c * Caug[:, 0:HD] + upd[:, 0:HD]
            Co_ref[bi, h, :, HD:2 * HD] = w_c * Caug[:, HD:2 * HD] + upd[:, HD:2 * HD]
            mo_ref[bi, hr, :] = m_new

            hn = hh * lax.rsqrt(jnp.mean(hh * hh, axis=-1, keepdims=True) + EPS)
            yb = sig_ref[rs, h * HD:(h + 1) * HD] * (hn * mnw_ref[:, h * HD:(h + 1) * HD])
            y_ref[rs, DC + h * HD:DC + (h + 1) * HD] = yb.astype(BF16)

    y = _dot(y_ref[...], wout_ref[...])
    xo_ref[...] = (x + _rms(y, gpost_ref[...])).reshape(BB, TT, D)


def _mixer(x, conv0, C0, m0, gpre, w_in, w_kt, w_g, b_g, conv_w, mnw, w_out, gpost, *, BB, TT, L):
    B, T, D = x.shape
    DC = conv0.shape[-1]
    H, HD = C0.shape[1], C0.shape[2]
    DM = H * HD
    rows = BB * TT
    seq = lambda s: pl.BlockSpec((BB,) + s, lambda b, t: (b,) + (0,) * len(s))
    xspec = pl.BlockSpec((BB, TT, D), lambda b, t: (b, t, 0))
    kern = functools.partial(_mixer_kernel, BB=BB, TT=TT, L=L, DC=DC, HD=HD)
    return pl.pallas_call(
        kern,
        grid=(B // BB, T // TT),
        in_specs=[xspec, seq((CONV_W - 1, DC)), seq((H, HD, 2 * HD)), seq((H, LANES)),
                  _const_spec((1, D)), _const_spec(w_in.shape), _const_spec(w_kt.shape), _const_spec(w_g.shape),
                  _const_spec(b_g.shape), _const_spec(conv_w.shape), _const_spec((1, DM)),
                  _const_spec(w_out.shape), _const_spec((1, D))],
        out_specs=[xspec, seq((CONV_W - 1, DC)), seq((H, HD, 2 * HD)), seq((H, LANES))],
        out_shape=[jax.ShapeDtypeStruct((B, T, D), F32),
                   jax.ShapeDtypeStruct((B, CONV_W - 1, DC), F32),
                   jax.ShapeDtypeStruct((B, H, HD, 2 * HD), F32),
                   jax.ShapeDtypeStruct((B, H, LANES), F32)],
        scratch_shapes=[pltpu.VMEM((BB, TT + CONV_PAD, DC), F32),
                        pltpu.VMEM((rows, DM), BF16),
                        pltpu.VMEM((DM, rows), BF16),
                        pltpu.VMEM((H, rows, 2 * HD), BF16),
                        pltpu.VMEM((rows, DM), F32),
                        pltpu.VMEM((H, rows, LANES), F32),
                        pltpu.VMEM((H, rows, LANES), F32),
                        pltpu.VMEM((rows // L, SUBLANES, L), F32),
                        pltpu.VMEM((rows, DC + DM), BF16)],
        compiler_params=pltpu.CompilerParams(
            dimension_semantics=("arbitrary", "arbitrary"), vmem_limit_bytes=VMEM_LIMIT_BYTES),
        name="token_mixer",
    )(x, conv0, C0, m0, gpre, w_in, w_kt, w_g, b_g, conv_w, mnw, w_out, gpost)


def _attn_kernel(x_ref, mk_ref, mv_ref, gpre_ref, wq_ref, wo_ref, gpost_ref, xo_ref,
                 q_ref, o_ref, *, BB, TT):
    rows = BB * TT
    D = x_ref.shape[-1]
    HD = D // MEM_HEADS
    x = x_ref[...].reshape(rows, D)
    xn = _rms(x, gpre_ref[...]).astype(BF16)
    q_ref[...] = (_dot(xn, wq_ref[...]) * (HD ** -0.5)).astype(BF16)

    for bi in range(BB):
        rs = pl.ds(bi * TT, TT)
        for h in range(MEM_HEADS):
            hs = slice(h * HD, (h + 1) * HD)
            s = _dot_nt(q_ref[rs, hs], mk_ref[bi, :, hs])
            e = jnp.exp(s - jnp.max(s, axis=-1, keepdims=True))
            p = e / jnp.sum(e, axis=-1, keepdims=True)
            o_ref[rs, hs] = _dot(p.astype(BF16), mv_ref[bi, :, hs]).astype(BF16)

    y = _dot(o_ref[...], wo_ref[...])
    xo_ref[...] = (x + _rms(y, gpost_ref[...])).reshape(BB, TT, D)


def _attn(x, mk, mv, gpre, wq, wo, gpost, *, BB, TT):
    B, T, D = x.shape
    M = mk.shape[1]
    xspec = pl.BlockSpec((BB, TT, D), lambda b, t: (b, t, 0))
    mspec = pl.BlockSpec((BB, M, D), lambda b, t: (b, 0, 0))
    kern = functools.partial(_attn_kernel, BB=BB, TT=TT)
    return pl.pallas_call(
        kern,
        grid=(B // BB, T // TT),
        in_specs=[xspec, mspec, mspec, _const_spec((1, D)), _const_spec((D, D)),
                  _const_spec((D, D)), _const_spec((1, D))],
        out_specs=xspec,
        out_shape=jax.ShapeDtypeStruct((B, T, D), F32),
        scratch_shapes=[pltpu.VMEM((BB * TT, D), BF16), pltpu.VMEM((BB * TT, D), BF16)],
        compiler_params=pltpu.CompilerParams(
            dimension_semantics=("arbitrary", "arbitrary"), vmem_limit_bytes=VMEM_LIMIT_BYTES),
        name="mem_attention",
    )(x, mk, mv, gpre, wq, wo, gpost)


def _ffn_kernel(x_ref, ff0_ref, gpre_ref, wup_ref, cw_ref, wdn_ref, gpost_ref,
                xo_ref, ffo_ref, xn_ref, ext_ref, act_ref, acc_ref, *, BB, TT, RB):
    rows = BB * TT
    D = x_ref.shape[-1]
    n_chunks, FC = wdn_ref.shape[0], wdn_ref.shape[1]
    lo = CONV_PAD - (CONV_W - 1)

    @pl.when(pl.program_id(1) == 0)
    def _():
        ffo_ref[...] = ff0_ref[...]

    def up(j, p):
        ext_ref[p, :, lo:CONV_PAD, :] = ffo_ref[j]
        ext_ref[p, :, CONV_PAD:CONV_PAD + TT, :] = _dot(xn_ref[...], wup_ref[j]).reshape(BB, TT, 2 * FC)
        ffo_ref[j] = ext_ref[p, :, TT + lo:TT + CONV_PAD, :]

    def conv(j, p, cols, bi, r0):
        y = ext_ref[p, bi, lo + r0:lo + r0 + RB, cols] * cw_ref[j, 0:1, cols]
        for k in range(1, CONV_W):
            y = y + ext_ref[p, bi, lo + k + r0:lo + k + r0 + RB, cols] * cw_ref[j, k:k + 1, cols]
        return y

    def gate(j, p):
        for bi in range(BB):
            for r0 in range(0, TT, RB):
                a = conv(j, p, slice(0, FC), bi, r0)
                g = conv(j, p, slice(FC, 2 * FC), bi, r0)
                t = jnp.tanh((2.0 / jnp.pi) ** 0.5 * (g + 0.044715 * (g * g * g)))
                act_ref[p, bi * TT + r0:bi * TT + r0 + RB, :] = ((g * a) * (1.0 + t)).astype(BF16)

    def down(j, p):
        acc_ref[...] += _dot(act_ref[p], wdn_ref[j])

    def stage(j, p):
        up(j + 1, 1 - p)
        gate(j, p)
        down(j - 1, 1 - p)

    xn_ref[...] = _rms(x_ref[...].reshape(rows, D), gpre_ref[...]).astype(BF16)
    acc_ref[...] = jnp.zeros_like(acc_ref)
    up(0, 0)
    up(1, 1)
    gate(0, 0)
    for j in range(1, n_chunks - 1):
        stage(j, j % 2)
    last = n_chunks - 1
    gate(last, last % 2)
    down(last - 1, (last - 1) % 2)
    down(last, last % 2)

    x = x_ref[...].reshape(rows, D)
    xo_ref[...] = (x + _rms(acc_ref[...], gpost_ref[...])).reshape(BB, TT, D)


def _ffn(x, ff0, gpre, w_up, conv_w, w_down, gpost, *, BB, TT, RB):
    B, T, D = x.shape
    n_chunks, FC, _ = w_down.shape
    xspec = pl.BlockSpec((BB, TT, D), lambda b, t: (b, t, 0))
    sspec = pl.BlockSpec((n_chunks, BB, CONV_W - 1, 2 * FC), lambda b, t: (0, b, 0, 0))
    kern = functools.partial(_ffn_kernel, BB=BB, TT=TT, RB=RB)
    return pl.pallas_call(
        kern,
        grid=(B // BB, T // TT),
        in_specs=[xspec, sspec, _const_spec((1, D)), _const_spec(w_up.shape),
                  _const_spec(conv_w.shape), _const_spec(w_down.shape), _const_spec((1, D))],
        out_specs=[xspec, sspec],
        out_shape=[jax.ShapeDtypeStruct((B, T, D), F32),
                   jax.ShapeDtypeStruct(ff0.shape, F32)],
        scratch_shapes=[pltpu.VMEM((BB * TT, D), BF16),
                        pltpu.VMEM((2, BB, TT + CONV_PAD, 2 * FC), F32),
                        pltpu.VMEM((2, BB * TT, FC), BF16),
                        pltpu.VMEM((BB * TT, D), F32)],
        compiler_params=pltpu.CompilerParams(
            dimension_semantics=("arbitrary", "arbitrary"), vmem_limit_bytes=VMEM_LIMIT_BYTES),
        name="conv_ffn",
    )(x, ff0, gpre, w_up, conv_w, w_down, gpost)


def _chunk_cols(w, n_chunks):
    lead = w.shape[:-1]
    w = w.reshape(lead + (2, n_chunks, -1))
    w = jnp.moveaxis(w, -2, 0)
    return w.reshape((n_chunks,) + lead + (-1,))


def _unchunk_cols(w):
    n_chunks, lead = w.shape[0], w.shape[1:-1]
    w = w.reshape((n_chunks,) + lead + (2, -1))
    w = jnp.moveaxis(w, 0, -2)
    return w.reshape(lead + (-1,))


def _tiles(T):
    TT = min(T, 512)
    return TT, min(TT, LANES)


def _layer(x, mk, mv, conv0, C0, n0, m0, ff0, p, *, BB):
    B, T, D = x.shape
    TT, L = _tiles(T)
    H = C0.shape[1]
    HD = C0.shape[-1]
    m0_rep = jnp.broadcast_to(m0[:, :, None], (B, H, LANES))
    C0_aug = jnp.concatenate([C0, jnp.broadcast_to(n0[..., None], C0.shape)], axis=-1)
    x, conv_new, C_aug, m_rep = _mixer(x, conv0, C0_aug, m0_rep, p['g_mix_pre'], p['w_in'], p['w_kt'], p['w_g'],
                                       p['b_g'], p['conv_w'], p['mnw'], p['w_out'], p['g_mix_post'],
                                       BB=BB, TT=TT, L=L)
    C, n = C_aug[..., 0:HD], C_aug[..., HD]
    x = _attn(x, mk.reshape(B, -1, D).astype(BF16), mv.reshape(B, -1, D).astype(BF16),
              p['g_mem_pre'], p['w_mq'], p['w_mo'], p['g_mem_post'], BB=BB, TT=TT)
    n_ff = p['w_down_half'].shape[0]
    x, ff_new = _ffn(x, _chunk_cols(ff0, n_ff), p['g_ffn_pre'], p['w_up'], p['ffn_conv_w'], p['w_down_half'],
                     p['g_ffn_post'], BB=BB, TT=TT, RB=min(TT, 64))
    return x, conv_new, C, n, m_rep[:, :, 0], _unchunk_cols(ff_new)


def kernel(x_prompt, x_sample, mem_prompt, cache_mem_k, cache_mem_v, state_conv, state_mlstm_C, state_mlstm_n, state_mlstm_m, state_ffn_conv, norm_mix_pre, w_in, b_gates, conv_w, mlstm_norm_w, w_out, norm_mix_post, norm_mem_pre, norm_mem_kv, w_mq, w_mk, w_mv, w_mo, norm_mem_post, norm_ffn_pre, w_up, ffn_conv_w, w_down, norm_ffn_post):
    depth = w_in.shape[0]
    B, _, D = x_prompt.shape
    DC = state_conv.shape[-1]
    H, HD = state_mlstm_C.shape[2], state_mlstm_C.shape[-1]
    DM = H * HD
    DFF2 = state_ffn_conv.shape[-1]
    n_main = 3 * DC + 4 * DM
    n_mem = mem_prompt.shape[1]
    n_ff = w_down.shape[1] // FFN_CHUNK

    xp, xs = x_prompt, x_sample
    outs = [[] for _ in range(12)]
    for l in range(depth):
        row = lambda a: a[l][None, :]
        w_g = jnp.zeros((D, 2 * LANES), F32)
        w_g = w_g.at[:, 0:H].set(w_in[l][:, n_main:n_main + H])
        w_g = w_g.at[:, LANES:LANES + H].set(w_in[l][:, n_main + H:n_main + 2 * H])
        b_g = jnp.zeros((1, 2 * LANES), F32)
        b_g = b_g.at[0, 0:H].set(b_gates[l][0:H]).at[0, LANES:LANES + H].set(b_gates[l][H:2 * H])
        p = {'g_mix_pre': row(norm_mix_pre), 'w_in': w_in[l][:, 0:n_main].astype(BF16),
             'w_kt': w_in[l][:, 3 * DC + DM:3 * DC + 2 * DM].T.astype(BF16),
             'w_g': w_g.astype(BF16), 'b_g': b_g, 'conv_w': conv_w[l], 'mnw': row(mlstm_norm_w),
             'w_out': w_out[l].astype(BF16), 'g_mix_post': row(norm_mix_post),
             'g_mem_pre': row(norm_mem_pre), 'w_mq': w_mq[l].astype(BF16), 'w_mo': w_mo[l].astype(BF16),
             'g_mem_post': row(norm_mem_post), 'g_ffn_pre': row(norm_ffn_pre),
             'w_up': _chunk_cols(w_up[l].astype(BF16), n_ff), 'ffn_conv_w': _chunk_cols(ffn_conv_w[l], n_ff),
             'w_down_half': (0.5 * w_down[l]).astype(BF16).reshape(n_ff, FFN_CHUNK, D),
             'g_ffn_post': row(norm_ffn_post)}

        mk_p, mv_p = _memkv(mem_prompt, row(norm_mem_kv), w_mk[l].astype(BF16), w_mv[l].astype(BF16))
        xp, conv_p, C_p, n_p, m_p, ff_p = _layer(
            xp, mk_p, mv_p,
            jnp.zeros((B, CONV_W - 1, DC), F32), jnp.zeros((B, H, HD, HD), F32),
            jnp.zeros((B, H, HD), F32), jnp.zeros((B, H), F32),
            jnp.zeros((B, CONV_W - 1, DFF2), F32), p, BB=1)
        xs, conv_s, C_s, n_s, m_s, ff_s = _layer(
            xs, cache_mem_k[l], cache_mem_v[l], state_conv[l], state_mlstm_C[l], state_mlstm_n[l],
            state_mlstm_m[l], state_ffn_conv[l], p, BB=8)
        heads = (B, n_mem, MEM_HEADS, D // MEM_HEADS)
        for lst, val in zip(outs, (mk_p.reshape(heads), mv_p.reshape(heads), conv_p, conv_s, C_p, C_s,
                                   n_p, n_s, m_p, m_s, ff_p, ff_s)):
            lst.append(val)
    return (xp, xs) + tuple(jnp.stack(o) for o in outs)
```

```python
import functools

import jax
import jax.numpy as jnp
from jax import lax
from jax.experimental import pallas as pl
from jax.experimental.pallas import tpu as pltpu

F32 = jnp.float32
BF16 = jnp.bfloat16

EPS = 1e-6
CONV_W = 3
MLSTM_HEADS = 4
MEM_HEADS = 4
LANES = 128
SUBLANES = 8
CONV_PAD = SUBLANES
FFN_CHUNK = 256
VMEM_LIMIT_BYTES = 56 * 1024 * 1024


def _rms(x, g):
    return x * lax.rsqrt(jnp.mean(x * x, axis=-1, keepdims=True) + EPS) * g


def _dot(a, b):
    return jnp.dot(a, b, preferred_element_type=F32)


def _dot_nt(a, b):
    return lax.dot_general(a, b, (((1,), (1,)), ((), ())), preferred_element_type=F32)


def _const_spec(shape):
    zeros = (0,) * len(shape)
    return pl.BlockSpec(shape, lambda b, t: zeros, pipeline_mode=pl.Buffered(1))


def _causal_conv(ext_ref, tail, u, w_ref, cols, TT):
    lo = CONV_PAD - (CONV_W - 1)
    ext_ref[:, lo:CONV_PAD, :] = tail
    ext_ref[:, CONV_PAD:CONV_PAD + TT, :] = u
    y = ext_ref[:, lo:lo + TT, :] * w_ref[0:1, cols]
    for j in range(1, CONV_W):
        y = y + ext_ref[:, lo + j:lo + j + TT, :] * w_ref[j:j + 1, cols]
    return y, ext_ref[:, TT + lo:TT + CONV_PAD, :]


def _memkv_kernel(mem_ref, g_ref, wk_ref, wv_ref, k_ref, v_ref, kb_ref, vb_ref):
    mn = _rms(mem_ref[0], g_ref[...]).astype(BF16)
    k = _dot(mn, wk_ref[...])
    v = _dot(mn, wv_ref[...])
    k_ref[0] = k
    v_ref[0] = v
    kb_ref[0] = k.astype(BF16)
    vb_ref[0] = v.astype(BF16)


def _memkv(mem, g, wk, wv):
    B, M, D = mem.shape
    spec = pl.BlockSpec((1, M, D), lambda b: (b, 0, 0))
    full = lambda s: pl.BlockSpec(s, lambda b: (0,) * len(s))
    return pl.pallas_call(
        _memkv_kernel,
        grid=(B,),
        in_specs=[spec, full((1, D)), full((D, D)), full((D, D))],
        out_specs=[spec] * 4,
        out_shape=[jax.ShapeDtypeStruct((B, M, D), F32)] * 2 + [jax.ShapeDtypeStruct((B, M, D), BF16)] * 2,
        compiler_params=pltpu.CompilerParams(
            dimension_semantics=("arbitrary",), vmem_limit_bytes=VMEM_LIMIT_BYTES),
        name="mem_kv",
    )(mem, g, wk, wv)


def _mixer_kernel(x_ref, conv0_ref, C0_ref, N0_ref, m0_ref, gpre_ref, win_ref, wkt_ref, wg_ref, bg_ref,
                  convw_ref, mnw_ref, wout_ref, gpost_ref,
                  xo_ref, convo_ref, Co_ref, No_ref, mo_ref,
                  ext_ref, q_ref, kt_ref, vaug_ref, sig_ref, bc_ref, cm_ref, at_ref, y_ref,
                  *, BB, TT, L, DC, HD):
    rows = BB * TT
    H = MLSTM_HEADS
    DM = H * HD
    D = x_ref.shape[-1]
    n_chunks = rows // L
    chunks_per_seq = TT // L

    @pl.when(pl.program_id(1) == 0)
    def _():
        convo_ref[...] = conv0_ref[...]
        Co_ref[...] = C0_ref[...]
        No_ref[...] = N0_ref[...]
        mo_ref[...] = m0_ref[...]

    x = x_ref[...].reshape(rows, D)
    xn = _rms(x, gpre_ref[...]).astype(BF16)

    pa = _dot(xn, win_ref[:, 0:3 * DC])
    u = (pa[:, 2 * DC:3 * DC] * pa[:, 0:DC]).reshape(BB, TT, DC)
    ya, tail = _causal_conv(ext_ref, convo_ref[...], u, convw_ref, slice(0, DC), TT)
    convo_ref[...] = tail
    y_ref[:, 0:DC] = (pa[:, DC:2 * DC] * ya.reshape(rows, DC)).astype(BF16)

    c0 = 3 * DC
    q_ref[...] = _dot(xn, win_ref[:, c0:c0 + DM]).astype(BF16)
    kt_ref[...] = (_dot_nt(wkt_ref[...], xn) * (HD ** -0.5)).astype(BF16)
    vo = _dot(xn, win_ref[:, c0 + 2 * DM:c0 + 4 * DM])
    for h in range(H):
        vaug_ref[h, :, 0:HD] = vo[:, h * HD:(h + 1) * HD].astype(BF16)
        vaug_ref[h, :, HD:2 * HD] = jnp.ones((rows, HD), BF16)
    sig_ref[...] = 1.0 / (1.0 + jnp.exp(-vo[:, DM:2 * DM]))

    g = _dot(xn, wg_ref[...]) + bg_ref[...]
    ipre = g[:, 0:LANES]
    fpre = g[:, LANES:2 * LANES]
    logf = jnp.minimum(fpre, 0.0) - jnp.log1p(jnp.exp(-jnp.abs(fpre)))
    pos = lax.broadcasted_iota(jnp.int32, (rows, LANES), 0) & (L - 1)
    bc = logf
    sh = 1
    while sh < L:
        bc = bc + jnp.where(pos >= sh, pltpu.roll(bc, sh, axis=0), 0.0)
        sh *= 2
    a = ipre - bc
    cm = a
    sh = 1
    while sh < L:
        cm = jnp.maximum(cm, jnp.where(pos >= sh, pltpu.roll(cm, sh, axis=0), -jnp.inf))
        sh *= 2
    a_t = a.T
    for c in range(n_chunks):
        at_ref[c] = a_t[0:SUBLANES, c * L:(c + 1) * L]
    for h in range(H):
        bc_ref[h] = jnp.broadcast_to(bc[:, h:h + 1], (rows, LANES))
        cm_ref[h] = jnp.broadcast_to(cm[:, h:h + 1], (rows, LANES))

    causal = (lax.broadcasted_iota(jnp.int32, (L, L), 0)
              >= lax.broadcasted_iota(jnp.int32, (L, L), 1))

    for ci in range(n_chunks):
        rs = pl.ds(ci * L, L)
        bi = ci // chunks_per_seq
        a_rows = at_ref[ci]
        for h in range(H):
            hr = pl.ds(h, 1)
            q = q_ref[rs, h * HD:(h + 1) * HD]
            kt = kt_ref[h * HD:(h + 1) * HD, ci * L:(ci + 1) * L]
            vaug = vaug_ref[h, rs, :]
            b_col = bc_ref[h, rs, :]
            a_row = a_rows[h:h + 1, :]
            C = Co_ref[bi, h]
            N = No_ref[bi, h]
            m_prev = mo_ref[bi, hr, :]

            m_t = b_col + jnp.maximum(cm_ref[h, rs, :], m_prev)
            d_log = jnp.where(causal, b_col[:, 0:L] + a_row, -jnp.inf)
            s = _dot(q, kt) * jnp.exp(d_log - m_t[:, 0:L])
            w_inter = jnp.exp(b_col + m_prev - m_t)
            intra = _dot(s.astype(BF16), vaug)
            inter = _dot(q, jnp.concatenate([C, N], axis=1).astype(BF16))
            num = intra[:, 0:HD] + w_inter * inter[:, 0:HD]
            den = intra[:, HD:2 * HD] + w_inter * inter[:, HD:2 * HD]
            hh = num / jnp.maximum(jnp.abs(den), jnp.exp(-m_t))

            m_new = m_t[L - 1:L, :]
            b_last = b_col[L - 1:L, :]
            w_s = jnp.exp(b_last[:, 0:L] + a_row - m_new[:, 0:L])
            w_c = jnp.exp(b_last + m_prev - m_new)
            kw = (kt.astype(F32) * w_s).astype(BF16)
            upd = _dot(kw, vaug)
            Co_ref[bi, h] = w_c * C + upd[:, 0:HD]
            No_ref[bi, h] = w_c * N + upd[:, HD:2 * HD]
            mo_ref[bi, hr, :] = m_new

            hn = hh * lax.rsqrt(jnp.mean(hh * hh, axis=-1, keepdims=True) + EPS)
            yb = sig_ref[rs, h * HD:(h + 1) * HD] * (hn * mnw_ref[:, h * HD:(h + 1) * HD])
            y_ref[rs, DC + h * HD:DC + (h + 1) * HD] = yb.astype(BF16)

    y = _dot(y_ref[...], wout_ref[...])
    xo_ref[...] = (x + _rms(y, gpost_ref[...])).reshape(BB, TT, D)


def _mixer(x, conv0, C0, N0, m0, gpre, w_in, w_kt, w_g, b_g, conv_w, mnw, w_out, gpost, *, BB, TT, L):
    B, T, D = x.shape
    DC = conv0.shape[-1]
    H, HD = C0.shape[1], C0.shape[2]
    DM = H * HD
    rows = BB * TT
    seq = lambda s: pl.BlockSpec((BB,) + s, lambda b, t: (b,) + (0,) * len(s))
    xspec = pl.BlockSpec((BB, TT, D), lambda b, t: (b, t, 0))
    kern = functools.partial(_mixer_kernel, BB=BB, TT=TT, L=L, DC=DC, HD=HD)
    return pl.pallas_call(
        kern,
        grid=(B // BB, T // TT),
        in_specs=[xspec, seq((CONV_W - 1, DC)), seq((H, HD, HD)), seq((H, HD, HD)), seq((H, LANES)),
                  _const_spec((1, D)), _const_spec(w_in.shape), _const_spec(w_kt.shape), _const_spec(w_g.shape),
                  _const_spec(b_g.shape), _const_spec(conv_w.shape), _const_spec((1, DM)),
                  _const_spec(w_out.shape), _const_spec((1, D))],
        out_specs=[xspec, seq((CONV_W - 1, DC)), seq((H, HD, HD)), seq((H, HD, HD)), seq((H, LANES))],
        out_shape=[jax.ShapeDtypeStruct((B, T, D), F32),
                   jax.ShapeDtypeStruct((B, CONV_W - 1, DC), F32),
                   jax.ShapeDtypeStruct((B, H, HD, HD), F32),
                   jax.ShapeDtypeStruct((B, H, HD, HD), F32),
                   jax.ShapeDtypeStruct((B, H, LANES), F32)],
        scratch_shapes=[pltpu.VMEM((BB, TT + CONV_PAD, DC), F32),
                        pltpu.VMEM((rows, DM), BF16),
                        pltpu.VMEM((DM, rows), BF16),
                        pltpu.VMEM((H, rows, 2 * HD), BF16),
                        pltpu.VMEM((rows, DM), F32),
                        pltpu.VMEM((H, rows, LANES), F32),
                        pltpu.VMEM((H, rows, LANES), F32),
                        pltpu.VMEM((rows // L, SUBLANES, L), F32),
                        pltpu.VMEM((rows, DC + DM), BF16)],
        compiler_params=pltpu.CompilerParams(
            dimension_semantics=("arbitrary", "arbitrary"), vmem_limit_bytes=VMEM_LIMIT_BYTES),
        name="token_mixer",
    )(x, conv0, C0, N0, m0, gpre, w_in, w_kt, w_g, b_g, conv_w, mnw, w_out, gpost)


def _attn_kernel(x_ref, mk_ref, mv_ref, gpre_ref, wq_ref, wo_ref, gpost_ref, xo_ref,
                 q_ref, o_ref, *, BB, TT):
    rows = BB * TT
    D = x_ref.shape[-1]
    HD = D // MEM_HEADS
    x = x_ref[...].reshape(rows, D)
    xn = _rms(x, gpre_ref[...]).astype(BF16)
    q_ref[...] = (_dot(xn, wq_ref[...]) * (HD ** -0.5)).astype(BF16)

    for bi in range(BB):
        rs = pl.ds(bi * TT, TT)
        for h in range(MEM_HEADS):
            hs = slice(h * HD, (h + 1) * HD)
            s = _dot_nt(q_ref[rs, hs], mk_ref[bi, :, hs].astype(BF16))
            e = jnp.exp(s - jnp.max(s, axis=-1, keepdims=True))
            p = e / jnp.sum(e, axis=-1, keepdims=True)
            o_ref[rs, hs] = _dot(p.astype(BF16), mv_ref[bi, :, hs].astype(BF16)).astype(BF16)

    y = _dot(o_ref[...], wo_ref[...])
    xo_ref[...] = (x + _rms(y, gpost_ref[...])).reshape(BB, TT, D)


def _attn(x, mk, mv, gpre, wq, wo, gpost, *, BB, TT):
    B, T, D = x.shape
    M = mk.shape[1]
    xspec = pl.BlockSpec((BB, TT, D), lambda b, t: (b, t, 0))
    mspec = pl.BlockSpec((BB, M, D), lambda b, t: (b, 0, 0))
    kern = functools.partial(_attn_kernel, BB=BB, TT=TT)
    return pl.pallas_call(
        kern,
        grid=(B // BB, T // TT),
        in_specs=[xspec, mspec, mspec, _const_spec((1, D)), _const_spec((D, D)),
                  _const_spec((D, D)), _const_spec((1, D))],
        out_specs=xspec,
        out_shape=jax.ShapeDtypeStruct((B, T, D), F32),
        scratch_shapes=[pltpu.VMEM((BB * TT, D), BF16), pltpu.VMEM((BB * TT, D), BF16)],
        compiler_params=pltpu.CompilerParams(
            dimension_semantics=("arbitrary", "arbitrary"), vmem_limit_bytes=VMEM_LIMIT_BYTES),
        name="mem_attention",
    )(x, mk, mv, gpre, wq, wo, gpost)


def _ffn_kernel(x_ref, ff0_ref, gpre_ref, wup_ref, cw_ref, wdn_ref, gpost_ref,
                xo_ref, ffo_ref, xn_ref, ext_ref, act_ref, acc_ref, *, BB, TT, FC, RB):
    rows = BB * TT
    D = x_ref.shape[-1]
    DFF = wdn_ref.shape[0]
    n_chunks = DFF // FC
    lo = CONV_PAD - (CONV_W - 1)

    @pl.when(pl.program_id(1) == 0)
    def _():
        ffo_ref[...] = ff0_ref[...]

    def cols(j, half):
        return slice(half * DFF + j * FC, half * DFF + (j + 1) * FC)

    def up(j, p):
        for half in range(2):
            c, win = cols(j, half), slice(half * FC, (half + 1) * FC)
            ext_ref[p, :, lo:CONV_PAD, win] = ffo_ref[:, :, c]
            ext_ref[p, :, CONV_PAD:CONV_PAD + TT, win] = _dot(xn_ref[...], wup_ref[:, c]).reshape(BB, TT, FC)
            ffo_ref[:, :, c] = ext_ref[p, :, TT + lo:TT + CONV_PAD, win]

    def conv(j, p, half, bi, r0):
        c, win = cols(j, half), slice(half * FC, (half + 1) * FC)
        y = ext_ref[p, bi, lo + r0:lo + r0 + RB, win] * cw_ref[0:1, c]
        for k in range(1, CONV_W):
            y = y + ext_ref[p, bi, lo + k + r0:lo + k + r0 + RB, win] * cw_ref[k:k + 1, c]
        return y

    def gate(j, p):
        for bi in range(BB):
            for r0 in range(0, TT, RB):
                a = conv(j, p, 0, bi, r0)
                g = conv(j, p, 1, bi, r0)
                t = jnp.tanh((2.0 / jnp.pi) ** 0.5 * (g + 0.044715 * (g * g * g)))
                act_ref[p, bi * TT + r0:bi * TT + r0 + RB, :] = ((g * a) * (1.0 + t)).astype(BF16)

    def down(j, p):
        acc_ref[...] += _dot(act_ref[p], wdn_ref[j * FC:(j + 1) * FC, :])

    def stage(j, p):
        up(j + 1, 1 - p)
        gate(j, p)
        down(j - 1, 1 - p)

    xn_ref[...] = _rms(x_ref[...].reshape(rows, D), gpre_ref[...]).astype(BF16)
    acc_ref[...] = jnp.zeros_like(acc_ref)
    up(0, 0)
    up(1, 1)
    gate(0, 0)
    for j in range(1, n_chunks - 1):
        stage(j, j % 2)
    last = n_chunks - 1
    gate(last, last % 2)
    down(last - 1, (last - 1) % 2)
    down(last, last % 2)

    x = x_ref[...].reshape(rows, D)
    xo_ref[...] = (x + _rms(acc_ref[...], gpost_ref[...])).reshape(BB, TT, D)


def _ffn(x, ff0, gpre, w_up, conv_w, w_down, gpost, *, BB, TT, FC, RB):
    B, T, D = x.shape
    DFF2 = w_up.shape[1]
    xspec = pl.BlockSpec((BB, TT, D), lambda b, t: (b, t, 0))
    sspec = pl.BlockSpec((BB, CONV_W - 1, DFF2), lambda b, t: (b, 0, 0))
    kern = functools.partial(_ffn_kernel, BB=BB, TT=TT, FC=FC, RB=RB)
    return pl.pallas_call(
        kern,
        grid=(B // BB, T // TT),
        in_specs=[xspec, sspec, _const_spec((1, D)), _const_spec(w_up.shape),
                  _const_spec(conv_w.shape), _const_spec(w_down.shape), _const_spec((1, D))],
        out_specs=[xspec, sspec],
        out_shape=[jax.ShapeDtypeStruct((B, T, D), F32),
                   jax.ShapeDtypeStruct(ff0.shape, F32)],
        scratch_shapes=[pltpu.VMEM((BB * TT, D), BF16),
                        pltpu.VMEM((2, BB, TT + CONV_PAD, 2 * FC), F32),
                        pltpu.VMEM((2, BB * TT, FC), BF16),
                        pltpu.VMEM((BB * TT, D), F32)],
        compiler_params=pltpu.CompilerParams(
            dimension_semantics=("arbitrary", "arbitrary"), vmem_limit_bytes=VMEM_LIMIT_BYTES),
        name="conv_ffn",
    )(x, ff0, gpre, w_up, conv_w, w_down, gpost)


def _tiles(T):
    TT = min(T, 512)
    return TT, min(TT, LANES)


def _layer(x, mk, mv, conv0, C0, n0, m0, ff0, p, *, BB):
    B, T, D = x.shape
    TT, L = _tiles(T)
    H = C0.shape[1]
    m0_rep = jnp.broadcast_to(m0[:, :, None], (B, H, LANES))
    N0 = jnp.broadcast_to(n0[..., None], C0.shape)
    x, conv_new, C, N, m_rep = _mixer(x, conv0, C0, N0, m0_rep, p['g_mix_pre'], p['w_in'], p['w_kt'], p['w_g'],
                                      p['b_g'], p['conv_w'], p['mnw'], p['w_out'], p['g_mix_post'],
                                      BB=BB, TT=TT, L=L)
    n = N[..., 0]
    x = _attn(x, mk, mv, p['g_mem_pre'], p['w_mq'], p['w_mo'], p['g_mem_post'], BB=BB, TT=TT)
    x, ff_new = _ffn(x, ff0, p['g_ffn_pre'], p['w_up'], p['ffn_conv_w'], p['w_down_half'],
                     p['g_ffn_post'], BB=BB, TT=TT, FC=FFN_CHUNK, RB=min(TT, 64))
    return x, conv_new, C, n, m_rep[:, :, 0], ff_new


def kernel(x_prompt, x_sample, mem_prompt, cache_mem_k, cache_mem_v, state_conv, state_mlstm_C, state_mlstm_n, state_mlstm_m, state_ffn_conv, norm_mix_pre, w_in, b_gates, conv_w, mlstm_norm_w, w_out, norm_mix_post, norm_mem_pre, norm_mem_kv, w_mq, w_mk, w_mv, w_mo, norm_mem_post, norm_ffn_pre, w_up, ffn_conv_w, w_down, norm_ffn_post):
    depth = w_in.shape[0]
    B, _, D = x_prompt.shape
    BS = x_sample.shape[0]
    DC = state_conv.shape[-1]
    H, HD = state_mlstm_C.shape[2], state_mlstm_C.shape[-1]
    DM = H * HD
    DFF2 = state_ffn_conv.shape[-1]
    n_main = 3 * DC + 4 * DM
    n_mem = mem_prompt.shape[1]

    xp, xs = x_prompt, x_sample
    outs = [[] for _ in range(12)]
    for l in range(depth):
        row = lambda a: a[l][None, :]
        w_g = jnp.zeros((D, 2 * LANES), F32)
        w_g = w_g.at[:, 0:H].set(w_in[l][:, n_main:n_main + H])
        w_g = w_g.at[:, LANES:LANES + H].set(w_in[l][:, n_main + H:n_main + 2 * H])
        b_g = jnp.zeros((1, 2 * LANES), F32)
        b_g = b_g.at[0, 0:H].set(b_gates[l][0:H]).at[0, LANES:LANES + H].set(b_gates[l][H:2 * H])
        p = {'g_mix_pre': row(norm_mix_pre), 'w_in': w_in[l][:, 0:n_main].astype(BF16),
             'w_kt': w_in[l][:, 3 * DC + DM:3 * DC + 2 * DM].T.astype(BF16),
             'w_g': w_g.astype(BF16), 'b_g': b_g, 'conv_w': conv_w[l], 'mnw': row(mlstm_norm_w),
             'w_out': w_out[l].astype(BF16), 'g_mix_post': row(norm_mix_post),
             'g_mem_pre': row(norm_mem_pre), 'w_mq': w_mq[l].astype(BF16), 'w_mo': w_mo[l].astype(BF16),
             'g_mem_post': row(norm_mem_post), 'g_ffn_pre': row(norm_ffn_pre),
             'w_up': w_up[l].astype(BF16), 'ffn_conv_w': ffn_conv_w[l],
             'w_down_half': (0.5 * w_down[l]).astype(BF16),
             'g_ffn_post': row(norm_ffn_post)}

        mk_p, mv_p, mk_pb, mv_pb = _memkv(mem_prompt, row(norm_mem_kv), w_mk[l].astype(BF16), w_mv[l].astype(BF16))
        xp, conv_p, C_p, n_p, m_p, ff_p = _layer(
            xp, mk_pb, mv_pb,
            jnp.zeros((B, CONV_W - 1, DC), F32), jnp.zeros((B, H, HD, HD), F32),
            jnp.zeros((B, H, HD), F32), jnp.zeros((B, H), F32),
            jnp.zeros((B, CONV_W - 1, DFF2), F32), p, BB=1)
        xs, conv_s, C_s, n_s, m_s, ff_s = _layer(
            xs, cache_mem_k[l].reshape(BS, n_mem, D), cache_mem_v[l].reshape(BS, n_mem, D), state_conv[l],
            state_mlstm_C[l], state_mlstm_n[l], state_mlstm_m[l], state_ffn_conv[l], p, BB=8)
        heads = (B, n_mem, MEM_HEADS, D // MEM_HEADS)
        for lst, val in zip(outs, (mk_p.reshape(heads), mv_p.reshape(heads), conv_p, conv_s, C_p, C_s,
                                   n_p, n_s, m_p, m_s, ff_p, ff_s)):
            lst.append(val)
    return (xp, xs) + tuple(jnp.stack(o) for o in outs)
```

```python
import functools

import jax
import jax.numpy as jnp
from jax import lax
from jax.experimental import pallas as pl
from jax.experimental.pallas import tpu as pltpu

F32 = jnp.float32
BF16 = jnp.bfloat16

EPS = 1e-6
CONV_W = 3
MLSTM_HEADS = 4
MEM_HEADS = 4
LANES = 128
SUBLANES = 8
CONV_PAD = SUBLANES
FFN_CHUNK = 256
VMEM_LIMIT_BYTES = 56 * 1024 * 1024


def _rms(x, g):
    return x * lax.rsqrt(jnp.mean(x * x, axis=-1, keepdims=True) + EPS) * g


def _dot(a, b):
    return jnp.dot(a, b, preferred_element_type=F32)


def _dot_nt(a, b):
    return lax.dot_general(a, b, (((1,), (1,)), ((), ())), preferred_element_type=F32)


def _const_spec(shape):
    zeros = (0,) * len(shape)
    return pl.BlockSpec(shape, lambda b, t: zeros, pipeline_mode=pl.Buffered(1))


def _causal_conv(ext_ref, tail, u, w_ref, cols, TT):
    lo = CONV_PAD - (CONV_W - 1)
    ext_ref[:, lo:CONV_PAD, :] = tail
    ext_ref[:, CONV_PAD:CONV_PAD + TT, :] = u
    y = ext_ref[:, lo:lo + TT, :] * w_ref[0:1, cols]
    for j in range(1, CONV_W):
        y = y + ext_ref[:, lo + j:lo + j + TT, :] * w_ref[j:j + 1, cols]
    return y, ext_ref[:, TT + lo:TT + CONV_PAD, :]


def _memkv_kernel(mem_ref, g_ref, wk_ref, wv_ref, k_ref, v_ref, kb_ref, vb_ref):
    mn = _rms(mem_ref[0], g_ref[...]).astype(BF16)
    k = _dot(mn, wk_ref[...])
    v = _dot(mn, wv_ref[...])
    k_ref[0] = k
    v_ref[0] = v
    kb_ref[0] = k.astype(BF16)
    vb_ref[0] = v.astype(BF16)


def _memkv(mem, g, wk, wv):
    B, M, D = mem.shape
    spec = pl.BlockSpec((1, M, D), lambda b: (b, 0, 0))
    full = lambda s: pl.BlockSpec(s, lambda b: (0,) * len(s))
    return pl.pallas_call(
        _memkv_kernel,
        grid=(B,),
        in_specs=[spec, full((1, D)), full((D, D)), full((D, D))],
        out_specs=[spec] * 4,
        out_shape=[jax.ShapeDtypeStruct((B, M, D), F32)] * 2 + [jax.ShapeDtypeStruct((B, M, D), BF16)] * 2,
        compiler_params=pltpu.CompilerParams(
            dimension_semantics=("arbitrary",), vmem_limit_bytes=VMEM_LIMIT_BYTES),
        name="mem_kv",
    )(mem, g, wk, wv)


def _mixer_kernel(x_ref, conv0_ref, C0_ref, n0_ref, m0_ref, gpre_ref, win_ref, wkt_ref, wg_ref, bg_ref,
                  convw_ref, mnw_ref, wout_ref, gpost_ref,
                  xo_ref, convo_ref, Co_ref, no_ref, mo_ref,
                  N_ref, ext_ref, q_ref, kt_ref, vaug_ref, sig_ref, bc_ref, cm_ref, at_ref, y_ref,
                  *, BB, TT, L, DC, HD):
    rows = BB * TT
    H = MLSTM_HEADS
    DM = H * HD
    D = x_ref.shape[-1]
    n_chunks = rows // L
    chunks_per_seq = TT // L

    @pl.when(pl.program_id(1) == 0)
    def _():
        convo_ref[...] = conv0_ref[...]
        Co_ref[...] = C0_ref[...]
        mo_ref[...] = m0_ref[...]
        for bi in range(BB):
            for h in range(H):
                N_ref[bi, h] = jnp.broadcast_to(n0_ref[bi, h:h + 1, :], (HD, HD)).T

    x = x_ref[...].reshape(rows, D)
    xn = _rms(x, gpre_ref[...]).astype(BF16)

    pa = _dot(xn, win_ref[:, 0:3 * DC])
    u = (pa[:, 2 * DC:3 * DC] * pa[:, 0:DC]).reshape(BB, TT, DC)
    ya, tail = _causal_conv(ext_ref, convo_ref[...], u, convw_ref, slice(0, DC), TT)
    convo_ref[...] = tail
    y_ref[:, 0:DC] = (pa[:, DC:2 * DC] * ya.reshape(rows, DC)).astype(BF16)

    c0 = 3 * DC
    q_ref[...] = _dot(xn, win_ref[:, c0:c0 + DM]).astype(BF16)
    kt_ref[...] = (_dot_nt(wkt_ref[...], xn) * (HD ** -0.5)).astype(BF16)
    vo = _dot(xn, win_ref[:, c0 + 2 * DM:c0 + 4 * DM])
    for h in range(H):
        vaug_ref[h, :, 0:HD] = vo[:, h * HD:(h + 1) * HD].astype(BF16)
        vaug_ref[h, :, HD:2 * HD] = jnp.ones((rows, HD), BF16)
    sig_ref[...] = 1.0 / (1.0 + jnp.exp(-vo[:, DM:2 * DM]))

    g = _dot(xn, wg_ref[...]) + bg_ref[...]
    ipre = g[:, 0:LANES]
    fpre = g[:, LANES:2 * LANES]
    logf = jnp.minimum(fpre, 0.0) - jnp.log1p(jnp.exp(-jnp.abs(fpre)))
    pos = lax.broadcasted_iota(jnp.int32, (rows, LANES), 0) & (L - 1)
    bc = logf
    sh = 1
    while sh < L:
        bc = bc + jnp.where(pos >= sh, pltpu.roll(bc, sh, axis=0), 0.0)
        sh *= 2
    a = ipre - bc
    cm = a
    sh = 1
    while sh < L:
        cm = jnp.maximum(cm, jnp.where(pos >= sh, pltpu.roll(cm, sh, axis=0), -jnp.inf))
        sh *= 2
    a_t = a.T
    for c in range(n_chunks):
        at_ref[c] = a_t[0:SUBLANES, c * L:(c + 1) * L]
    for h in range(H):
        bc_ref[h] = jnp.broadcast_to(bc[:, h:h + 1], (rows, LANES))
        cm_ref[h] = jnp.broadcast_to(cm[:, h:h + 1], (rows, LANES))

    causal = (lax.broadcasted_iota(jnp.int32, (L, L), 0)
              >= lax.broadcasted_iota(jnp.int32, (L, L), 1))

    for ci in range(n_chunks):
        rs = pl.ds(ci * L, L)
        bi = ci // chunks_per_seq
        a_rows = at_ref[ci]
        for h in range(H):
            hr = pl.ds(h, 1)
            q = q_ref[rs, h * HD:(h + 1) * HD]
            kt = kt_ref[h * HD:(h + 1) * HD, ci * L:(ci + 1) * L]
            vaug = vaug_ref[h, rs, :]
            b_col = bc_ref[h, rs, :]
            a_row = a_rows[h:h + 1, :]
            C = Co_ref[bi, h]
            N = N_ref[bi, h]
            m_prev = mo_ref[bi, hr, :]

            m_t = b_col + jnp.maximum(cm_ref[h, rs, :], m_prev)
            d_log = jnp.where(causal, b_col[:, 0:L] + a_row, -jnp.inf)
            s = _dot(q, kt) * jnp.exp(d_log - m_t[:, 0:L])
            w_inter = jnp.exp(b_col + m_prev - m_t)
            intra = _dot(s.astype(BF16), vaug)
            inter = _dot(q, jnp.concatenate([C, N], axis=1).astype(BF16))
            num = intra[:, 0:HD] + w_inter * inter[:, 0:HD]
            den = intra[:, HD:2 * HD] + w_inter * inter[:, HD:2 * HD]
            hh = num / jnp.maximum(jnp.abs(den), jnp.exp(-m_t))

            m_new = m_t[L - 1:L, :]
            b_last = b_col[L - 1:L, :]
            w_s = jnp.exp(b_last[:, 0:L] + a_row - m_new[:, 0:L])
            w_c = jnp.exp(b_last + m_prev - m_new)
            kw = (kt.astype(F32) * w_s).astype(BF16)
            upd = _dot(kw, vaug)
            Co_ref[bi, h] = w_c * C + upd[:, 0:HD]
            N_ref[bi, h] = w_c * N + upd[:, HD:2 * HD]
            mo_ref[bi, hr, :] = m_new

            hn = hh * lax.rsqrt(jnp.mean(hh * hh, axis=-1, keepdims=True) + EPS)
            yb = sig_ref[rs, h * HD:(h + 1) * HD] * (hn * mnw_ref[:, h * HD:(h + 1) * HD])
            y_ref[rs, DC + h * HD:DC + (h + 1) * HD] = yb.astype(BF16)

    y = _dot(y_ref[...], wout_ref[...])
    xo_ref[...] = (x + _rms(y, gpost_ref[...])).reshape(BB, TT, D)

    @pl.when(pl.program_id(1) == pl.num_programs(1) - 1)
    def _():
        for bi in range(BB):
            for h in range(H):
                no_ref[bi, h:h + 1, :] = N_ref[bi, h].T[0:1, :]


def _mixer(x, conv0, C0, n0, m0, gpre, w_in, w_kt, w_g, b_g, conv_w, mnw, w_out, gpost, *, BB, TT, L):
    B, T, D = x.shape
    DC = conv0.shape[-1]
    H, HD = C0.shape[1], C0.shape[2]
    DM = H * HD
    rows = BB * TT
    seq = lambda s: pl.BlockSpec((BB,) + s, lambda b, t: (b,) + (0,) * len(s))
    xspec = pl.BlockSpec((BB, TT, D), lambda b, t: (b, t, 0))
    kern = functools.partial(_mixer_kernel, BB=BB, TT=TT, L=L, DC=DC, HD=HD)
    return pl.pallas_call(
        kern,
        grid=(B // BB, T // TT),
        in_specs=[xspec, seq((CONV_W - 1, DC)), seq((H, HD, HD)), seq((H, HD)), seq((H, LANES)),
                  _const_spec((1, D)), _const_spec(w_in.shape), _const_spec(w_kt.shape), _const_spec(w_g.shape),
                  _const_spec(b_g.shape), _const_spec(conv_w.shape), _const_spec((1, DM)),
                  _const_spec(w_out.shape), _const_spec((1, D))],
        out_specs=[xspec, seq((CONV_W - 1, DC)), seq((H, HD, HD)), seq((H, HD)), seq((H, LANES))],
        out_shape=[jax.ShapeDtypeStruct((B, T, D), F32),
                   jax.ShapeDtypeStruct((B, CONV_W - 1, DC), F32),
                   jax.ShapeDtypeStruct((B, H, HD, HD), F32),
                   jax.ShapeDtypeStruct((B, H, HD), F32),
                   jax.ShapeDtypeStruct((B, H, LANES), F32)],
        scratch_shapes=[pltpu.VMEM((BB, H, HD, HD), F32),
                        pltpu.VMEM((BB, TT + CONV_PAD, DC), F32),
                        pltpu.VMEM((rows, DM), BF16),
                        pltpu.VMEM((DM, rows), BF16),
                        pltpu.VMEM((H, rows, 2 * HD), BF16),
                        pltpu.VMEM((rows, DM), F32),
                        pltpu.VMEM((H, rows, LANES), F32),
                        pltpu.VMEM((H, rows, LANES), F32),
                        pltpu.VMEM((rows // L, SUBLANES, L), F32),
                        pltpu.VMEM((rows, DC + DM), BF16)],
        compiler_params=pltpu.CompilerParams(
            dimension_semantics=("arbitrary", "arbitrary"), vmem_limit_bytes=VMEM_LIMIT_BYTES),
        name="token_mixer",
    )(x, conv0, C0, n0, m0, gpre, w_in, w_kt, w_g, b_g, conv_w, mnw, w_out, gpost)


def _attn_kernel(x_ref, mk_ref, mv_ref, gpre_ref, wq_ref, wo_ref, gpost_ref, xo_ref,
                 q_ref, o_ref, *, BB, TT):
    rows = BB * TT
    D = x_ref.shape[-1]
    HD = D // MEM_HEADS
    x = x_ref[...].reshape(rows, D)
    xn = _rms(x, gpre_ref[...]).astype(BF16)
    q_ref[...] = (_dot(xn, wq_ref[...]) * (HD ** -0.5)).astype(BF16)

    for bi in range(BB):
        rs = pl.ds(bi * TT, TT)
        for h in range(MEM_HEADS):
            hs = slice(h * HD, (h + 1) * HD)
            s = _dot_nt(q_ref[rs, hs], mk_ref[bi, :, hs].astype(BF16))
            e = jnp.exp(s - jnp.max(s, axis=-1, keepdims=True))
            p = e / jnp.sum(e, axis=-1, keepdims=True)
            o_ref[rs, hs] = _dot(p.astype(BF16), mv_ref[bi, :, hs].astype(BF16)).astype(BF16)

    y = _dot(o_ref[...], wo_ref[...])
    xo_ref[...] = (x + _rms(y, gpost_ref[...])).reshape(BB, TT, D)


def _attn(x, mk, mv, gpre, wq, wo, gpost, *, BB, TT):
    B, T, D = x.shape
    M = mk.shape[1]
    xspec = pl.BlockSpec((BB, TT, D), lambda b, t: (b, t, 0))
    mspec = pl.BlockSpec((BB, M, D), lambda b, t: (b, 0, 0))
    kern = functools.partial(_attn_kernel, BB=BB, TT=TT)
    return pl.pallas_call(
        kern,
        grid=(B // BB, T // TT),
        in_specs=[xspec, mspec, mspec, _const_spec((1, D)), _const_spec((D, D)),
                  _const_spec((D, D)), _const_spec((1, D))],
        out_specs=xspec,
        out_shape=jax.ShapeDtypeStruct((B, T, D), F32),
        scratch_shapes=[pltpu.VMEM((BB * TT, D), BF16), pltpu.VMEM((BB * TT, D), BF16)],
        compiler_params=pltpu.CompilerParams(
            dimension_semantics=("arbitrary", "arbitrary"), vmem_limit_bytes=VMEM_LIMIT_BYTES),
        name="mem_attention",
    )(x, mk, mv, gpre, wq, wo, gpost)


def _ffn_kernel(x_ref, ff0_ref, gpre_ref, wup_ref, cw_ref, wdn_ref, gpost_ref,
                xo_ref, ffo_ref, xn_ref, ext_ref, act_ref, acc_ref, *, BB, TT, FC, RB):
    rows = BB * TT
    D = x_ref.shape[-1]
    DFF = wdn_ref.shape[0]
    n_chunks = DFF // FC
    lo = CONV_PAD - (CONV_W - 1)

    @pl.when(pl.program_id(1) == 0)
    def _():
        ffo_ref[...] = ff0_ref[...]

    def cols(j, half):
        return slice(half * DFF + j * FC, half * DFF + (j + 1) * FC)

    def up(j, p):
        for half in range(2):
            c, win = cols(j, half), slice(half * FC, (half + 1) * FC)
            ext_ref[p, :, lo:CONV_PAD, win] = ffo_ref[:, :, c]
            ext_ref[p, :, CONV_PAD:CONV_PAD + TT, win] = _dot(xn_ref[...], wup_ref[:, c]).reshape(BB, TT, FC)
            ffo_ref[:, :, c] = ext_ref[p, :, TT + lo:TT + CONV_PAD, win]

    def conv(j, p, half, bi, r0):
        c, win = cols(j, half), slice(half * FC, (half + 1) * FC)
        y = ext_ref[p, bi, lo + r0:lo + r0 + RB, win] * cw_ref[0:1, c]
        for k in range(1, CONV_W):
            y = y + ext_ref[p, bi, lo + k + r0:lo + k + r0 + RB, win] * cw_ref[k:k + 1, c]
        return y

    def gate(j, p):
        for bi in range(BB):
            for r0 in range(0, TT, RB):
                a = conv(j, p, 0, bi, r0)
                g = conv(j, p, 1, bi, r0)
                t = jnp.tanh((2.0 / jnp.pi) ** 0.5 * (g + 0.044715 * (g * g * g)))
                act_ref[p, bi * TT + r0:bi * TT + r0 + RB, :] = ((g * a) * (1.0 + t)).astype(BF16)

    def down(j, p):
        acc_ref[...] += _dot(act_ref[p], wdn_ref[j * FC:(j + 1) * FC, :])

    def stage(j, p):
        up(j + 1, 1 - p)
        gate(j, p)
        down(j - 1, 1 - p)

    xn_ref[...] = _rms(x_ref[...].reshape(rows, D), gpre_ref[...]).astype(BF16)
    acc_ref[...] = jnp.zeros_like(acc_ref)
    up(0, 0)
    up(1, 1)
    gate(0, 0)
    for j in range(1, n_chunks - 1):
        stage(j, j % 2)
    last = n_chunks - 1
    gate(last, last % 2)
    down(last - 1, (last - 1) % 2)
    down(last, last % 2)

    x = x_ref[...].reshape(rows, D)
    xo_ref[...] = (x + _rms(acc_ref[...], gpost_ref[...])).reshape(BB, TT, D)


def _ffn(x, ff0, gpre, w_up, conv_w, w_down, gpost, *, BB, TT, FC, RB):
    B, T, D = x.shape
    DFF2 = w_up.shape[1]
    xspec = pl.BlockSpec((BB, TT, D), lambda b, t: (b, t, 0))
    sspec = pl.BlockSpec((BB, CONV_W - 1, DFF2), lambda b, t: (b, 0, 0))
    kern = functools.partial(_ffn_kernel, BB=BB, TT=TT, FC=FC, RB=RB)
    return pl.pallas_call(
        kern,
        grid=(B // BB, T // TT),
        in_specs=[xspec, sspec, _const_spec((1, D)), _const_spec(w_up.shape),
                  _const_spec(conv_w.shape), _const_spec(w_down.shape), _const_spec((1, D))],
        out_specs=[xspec, sspec],
        out_shape=[jax.ShapeDtypeStruct((B, T, D), F32),
                   jax.ShapeDtypeStruct(ff0.shape, F32)],
        scratch_shapes=[pltpu.VMEM((BB * TT, D), BF16),
                        pltpu.VMEM((2, BB, TT + CONV_PAD, 2 * FC), F32),
                        pltpu.VMEM((2, BB * TT, FC), BF16),
                        pltpu.VMEM((BB * TT, D), F32)],
        compiler_params=pltpu.CompilerParams(
            dimension_semantics=("arbitrary", "arbitrary"), vmem_limit_bytes=VMEM_LIMIT_BYTES),
        name="conv_ffn",
    )(x, ff0, gpre, w_up, conv_w, w_down, gpost)


def _tiles(T):
    TT = min(T, 512)
    return TT, min(TT, LANES)


def _layer(x, mk, mv, conv0, C0, n0, m0, ff0, p, *, BB):
    B, T, D = x.shape
    TT, L = _tiles(T)
    H = C0.shape[1]
    m0_rep = jnp.broadcast_to(m0[:, :, None], (B, H, LANES))
    x, conv_new, C, n, m_rep = _mixer(x, conv0, C0, n0, m0_rep, p['g_mix_pre'], p['w_in'], p['w_kt'], p['w_g'],
                                      p['b_g'], p['conv_w'], p['mnw'], p['w_out'], p['g_mix_post'],
                                      BB=BB, TT=TT, L=L)
    x = _attn(x, mk, mv, p['g_mem_pre'], p['w_mq'], p['w_mo'], p['g_mem_post'], BB=BB, TT=TT)
    x, ff_new = _ffn(x, ff0, p['g_ffn_pre'], p['w_up'], p['ffn_conv_w'], p['w_down_half'],
                     p['g_ffn_post'], BB=BB, TT=TT, FC=FFN_CHUNK, RB=min(TT, 64))
    return x, conv_new, C, n, m_rep[:, :, 0], ff_new


def kernel(x_prompt, x_sample, mem_prompt, cache_mem_k, cache_mem_v, state_conv, state_mlstm_C, state_mlstm_n, state_mlstm_m, state_ffn_conv, norm_mix_pre, w_in, b_gates, conv_w, mlstm_norm_w, w_out, norm_mix_post, norm_mem_pre, norm_mem_kv, w_mq, w_mk, w_mv, w_mo, norm_mem_post, norm_ffn_pre, w_up, ffn_conv_w, w_down, norm_ffn_post):
    depth = w_in.shape[0]
    B, _, D = x_prompt.shape
    BS = x_sample.shape[0]
    DC = state_conv.shape[-1]
    H, HD = state_mlstm_C.shape[2], state_mlstm_C.shape[-1]
    DM = H * HD
    DFF2 = state_ffn_conv.shape[-1]
    n_main = 3 * DC + 4 * DM
    n_mem = mem_prompt.shape[1]

    xp, xs = x_prompt, x_sample
    outs = [[] for _ in range(12)]
    for l in range(depth):
        row = lambda a: a[l][None, :]
        w_g = jnp.zeros((D, 2 * LANES), F32)
        w_g = w_g.at[:, 0:H].set(w_in[l][:, n_main:n_main + H])
        w_g = w_g.at[:, LANES:LANES + H].set(w_in[l][:, n_main + H:n_main + 2 * H])
        b_g = jnp.zeros((1, 2 * LANES), F32)
        b_g = b_g.at[0, 0:H].set(b_gates[l][0:H]).at[0, LANES:LANES + H].set(b_gates[l][H:2 * H])
        p = {'g_mix_pre': row(norm_mix_pre), 'w_in': w_in[l][:, 0:n_main].astype(BF16),
             'w_kt': w_in[l][:, 3 * DC + DM:3 * DC + 2 * DM].T.astype(BF16),
             'w_g': w_g.astype(BF16), 'b_g': b_g, 'conv_w': conv_w[l], 'mnw': row(mlstm_norm_w),
             'w_out': w_out[l].astype(BF16), 'g_mix_post': row(norm_mix_post),
             'g_mem_pre': row(norm_mem_pre), 'w_mq': w_mq[l].astype(BF16), 'w_mo': w_mo[l].astype(BF16),
             'g_mem_post': row(norm_mem_post), 'g_ffn_pre': row(norm_ffn_pre),
             'w_up': w_up[l].astype(BF16), 'ffn_conv_w': ffn_conv_w[l],
             'w_down_half': (0.5 * w_down[l]).astype(BF16),
             'g_ffn_post': row(norm_ffn_post)}

        mk_p, mv_p, mk_pb, mv_pb = _memkv(mem_prompt, row(norm_mem_kv), w_mk[l].astype(BF16), w_mv[l].astype(BF16))
        xp, conv_p, C_p, n_p, m_p, ff_p = _layer(
            xp, mk_pb, mv_pb,
            jnp.zeros((B, CONV_W - 1, DC), F32), jnp.zeros((B, H, HD, HD), F32),
            jnp.zeros((B, H, HD), F32), jnp.zeros((B, H), F32),
            jnp.zeros((B, CONV_W - 1, DFF2), F32), p, BB=1)
        xs, conv_s, C_s, n_s, m_s, ff_s = _layer(
            xs, cache_mem_k[l].reshape(BS, n_mem, D), cache_mem_v[l].reshape(BS, n_mem, D), state_conv[l],
            state_mlstm_C[l], state_mlstm_n[l], state_mlstm_m[l], state_ffn_conv[l], p, BB=8)
        heads = (B, n_mem, MEM_HEADS, D // MEM_HEADS)
        for lst, val in zip(outs, (mk_p.reshape(heads), mv_p.reshape(heads), conv_p, conv_s, C_p, C_s,
                                   n_p, n_s, m_p, m_s, ff_p, ff_s)):
            lst.append(val)
    return (xp, xs) + tuple(jnp.stack(o) for o in outs)
```
